```python
import math
import jax, jax.numpy as jnp
from jax import lax
import numpy as np

D_MODEL = 1024
BATCH = 8
SEQ = 4096
DEPTH = 4

CHUNK = 64
QBLOCK = 128
ROPE_THETA = 10000.0
NORM_EPS = 1e-6
N_MIXERS = 3

A_HEADS = 8
A_QK_DIM = 64
A_V_DIM = 2 * A_QK_DIM
A_WIDTH = A_HEADS * A_V_DIM
A_QK_WIDTH = A_HEADS * 2 * A_QK_DIM
A_IN = 2 * A_QK_WIDTH + A_WIDTH + A_WIDTH

B_WIDTH = D_MODEL
B_CONV = 31
B_IN = 3 * B_WIDTH

C_HEADS = 8
C_NOPE = 128
C_ROPE = 64
C_QK = C_NOPE + C_ROPE
C_V = 128
C_Q_LORA = 256
C_KV_LORA = 128
C_WIDTH = C_HEADS * C_V
C_IN = C_Q_LORA + C_KV_LORA + C_ROPE + C_WIDTH

N_A = (DEPTH + 2) // 3
N_B = (DEPTH + 1) // 3
N_C = DEPTH // 3

kernel_name = 'hybrid_diffattn_conformerconv_mla_chunkcausal'


def _rms_norm(x, g):
    xf = x.astype(jnp.float32)
    y = xf * lax.rsqrt(jnp.mean(xf * xf, axis=-1, keepdims=True) + NORM_EPS)
    return (y * g.astype(jnp.float32)).astype(x.dtype)


def _layer_norm(x, g, b):
    xf = x.astype(jnp.float32)
    mu = jnp.mean(xf, axis=-1, keepdims=True)
    xc = xf - mu
    y = xc * lax.rsqrt(jnp.mean(xc * xc, axis=-1, keepdims=True) + NORM_EPS)
    return (y * g.astype(jnp.float32) + b.astype(jnp.float32)).astype(x.dtype)


def _rope_cos_sin(seq, dim):
    inv = 1.0 / (ROPE_THETA ** (jnp.arange(0, dim, 2, dtype=jnp.float32) / dim))
    ang = jnp.arange(seq, dtype=jnp.float32)[:, None] * inv[None, :]
    return jnp.cos(ang), jnp.sin(ang)


def _apply_rope(x, cos, sin):
    xf = x.astype(jnp.float32)
    x1, x2 = jnp.split(xf, 2, axis=-1)
    return jnp.concatenate([x1 * cos - x2 * sin, x2 * cos + x1 * sin], axis=-1).astype(x.dtype)


def _chunk_causal_softmax(s, q_start):
    qb, kl = s.shape[-2], s.shape[-1]
    q_chunk = (q_start + jnp.arange(qb)) // CHUNK
    k_chunk = jnp.arange(kl) // CHUNK
    allowed = k_chunk[None, :] <= q_chunk[:, None]
    return jax.nn.softmax(jnp.where(allowed, s, -jnp.inf), axis=-1)


def _diff_attention(h, w_in, q_g, k_g, lq1, lk1, lq2, lk2, sub_g, w_out, lambda_init):
    bsz, seq, _ = h.shape
    proj = h @ w_in
    q, k, v, gate = jnp.split(proj, [A_QK_WIDTH, 2 * A_QK_WIDTH, 2 * A_QK_WIDTH + A_WIDTH], axis=-1)
    q = q.reshape(bsz, seq, A_HEADS, 2, A_QK_DIM)
    k = k.reshape(bsz, seq, A_HEADS, 2, A_QK_DIM)
    v = v.reshape(bsz, seq, A_HEADS, A_V_DIM).transpose(0, 2, 1, 3).astype(jnp.float32)
    cos, sin = _rope_cos_sin(seq, A_QK_DIM)
    cos, sin = cos[:, None, None, :], sin[:, None, None, :]
    q = _apply_rope(_rms_norm(q, q_g), cos, sin).transpose(0, 2, 3, 1, 4).astype(jnp.float32)
    k = _apply_rope(_rms_norm(k, k_g), cos, sin).transpose(0, 2, 3, 1, 4).astype(jnp.float32)
    f32 = jnp.float32
    lam = (jnp.exp(jnp.sum(lq1.astype(f32) * lk1.astype(f32)))
           - jnp.exp(jnp.sum(lq2.astype(f32) * lk2.astype(f32))) + lambda_init)
    scale = A_QK_DIM ** -0.5
    outs = []
    for s0 in range(0, seq, QBLOCK):
        e = s0 + QBLOCK
        s = jnp.einsum('bhmqd,bhmkd->bhmqk', q[:, :, :, s0:e], k[:, :, :, :e]) * scale
        p = _chunk_causal_softmax(s, s0)
        wts = p[:, :, 0] - lam * p[:, :, 1]
        outs.append(jnp.einsum('bhqk,bhkd->bhqd', wts, v[:, :, :e]))
    o = jnp.concatenate(outs, axis=2)
    o = _rms_norm(o, sub_g) * (1.0 - lambda_init)
    o = o.transpose(0, 2, 1, 3).reshape(bsz, seq, A_WIDTH).astype(h.dtype)
    return (o * jax.nn.silu(gate)) @ w_out


def _conformer_conv(h, w_in, b_in, conv_w, conv_b, ln_g, ln_b, w_out):
    proj = h @ w_in + b_in
    a, b, gate = jnp.split(proj, 3, axis=-1)
    u = a * jax.nn.sigmoid(b)
    u = lax.conv_general_dilated(
        u, conv_w[:, None, :].astype(u.dtype), window_strides=(1,),
        padding=[(B_CONV - 1, 0)], dimension_numbers=('NWC', 'WIO', 'NWC'),
        feature_group_count=B_WIDTH) + conv_b
    u = jax.nn.silu(_layer_norm(u, ln_g, ln_b))
    return (u * jax.nn.silu(gate)) @ w_out


def _mla(h, w_in, cq_g, w_uq, ckv_g, w_ukv, q_g, k_g, w_out):
    bsz, seq, _ = h.shape
    proj = h @ w_in
    c_q, c_kv, k_rope, gate = jnp.split(
        proj, [C_Q_LORA, C_Q_LORA + C_KV_LORA, C_Q_LORA + C_KV_LORA + C_ROPE], axis=-1)
    q = (_rms_norm(c_q, cq_g) @ w_uq).reshape(bsz, seq, C_HEADS, C_QK)
    kv = (_rms_norm(c_kv, ckv_g) @ w_ukv).reshape(bsz, seq, C_HEADS, C_NOPE + C_V)
    q_nope, q_rope = jnp.split(q, [C_NOPE], axis=-1)
    k_nope, v = jnp.split(kv, [C_NOPE], axis=-1)
    cos, sin = _rope_cos_sin(seq, C_ROPE)
    q_nope = _rms_norm(q_nope, q_g[:C_NOPE])
    q_rope = _apply_rope(_rms_norm(q_rope, q_g[C_NOPE:]), cos[:, None, :], sin[:, None, :])
    k_nope = _rms_norm(k_nope, k_g[:C_NOPE])
    k_rope = _apply_rope(_rms_norm(k_rope, k_g[C_NOPE:]), cos, sin).astype(jnp.float32)
    qn = q_nope.transpose(0, 2, 1, 3).astype(jnp.float32)
    qr = q_rope.transpose(0, 2, 1, 3).astype(jnp.float32)
    kn = k_nope.transpose(0, 2, 1, 3).astype(jnp.float32)
    vv = v.transpose(0, 2, 1, 3).astype(jnp.float32)
    scale = C_QK ** -0.5
    outs = []
    for s0 in range(0, seq, QBLOCK):
        e = s0 + QBLOCK
        s = (jnp.einsum('bhqd,bhkd->bhqk', qn[:, :, s0:e], kn[:, :, :e])
             + jnp.einsum('bhqr,bkr->bhqk', qr[:, :, s0:e], k_rope[:, :e])) * scale
        p = _chunk_causal_softmax(s, s0)
        outs.append(jnp.einsum('bhqk,bhkd->bhqd', p, vv[:, :, :e]))
    o = jnp.concatenate(outs, axis=2).transpose(0, 2, 1, 3).reshape(bsz, seq, C_WIDTH).astype(h.dtype)
    return (o * jax.nn.silu(gate)) @ w_out


def setup_inputs(seed: int = 0) -> dict:
    key = jax.random.key(seed)
    keys = iter(jax.random.split(key, 40))

    def nrm(shape, scale):
        return jax.random.normal(next(keys), shape, dtype=jnp.float32) * scale

    def gain(shape):
        return 1.0 + nrm(shape, 0.02)

    out_scale = 0.5
    return {
        'x': nrm((BATCH, SEQ, D_MODEL), 1.0),
        'a_norm_g': gain((N_A, D_MODEL)),
        'a_w_in': nrm((N_A, D_MODEL, A_IN), D_MODEL ** -0.5),
        'a_q_norm_g': gain((N_A, A_QK_DIM)),
        'a_k_norm_g': gain((N_A, A_QK_DIM)),
        'a_lam_q1': nrm((N_A, A_QK_DIM), 0.1),
        'a_lam_k1': nrm((N_A, A_QK_DIM), 0.1),
        'a_lam_q2': nrm((N_A, A_QK_DIM), 0.1),
        'a_lam_k2': nrm((N_A, A_QK_DIM), 0.1),
        'a_sub_norm_g': gain((N_A, A_V_DIM)),
        'a_w_out': nrm((N_A, A_WIDTH, D_MODEL), out_scale * A_WIDTH ** -0.5),
        'b_norm_g': gain((N_B, D_MODEL)),
        'b_w_in': nrm((N_B, D_MODEL, B_IN), D_MODEL ** -0.5),
        'b_b_in': nrm((N_B, B_IN), 0.01),
        'b_conv_w': nrm((N_B, B_CONV, B_WIDTH), B_CONV ** -0.5),
        'b_conv_b': nrm((N_B, B_WIDTH), 0.01),
        'b_ln_g': gain((N_B, B_WIDTH)),
        'b_ln_b': nrm((N_B, B_WIDTH), 0.01),
        'b_w_out': nrm((N_B, B_WIDTH, D_MODEL), out_scale * B_WIDTH ** -0.5),
        'c_norm_g': gain((N_C, D_MODEL)),
        'c_w_in': nrm((N_C, D_MODEL, C_IN), D_MODEL ** -0.5),
        'c_cq_norm_g': gain((N_C, C_Q_LORA)),
        'c_w_uq': nrm((N_C, C_Q_LORA, C_HEADS * C_QK), C_Q_LORA ** -0.5),
        'c_ckv_norm_g': gain((N_C, C_KV_LORA)),
        'c_w_ukv': nrm((N_C, C_KV_LORA, C_HEADS * (C_NOPE + C_V)), C_KV_LORA ** -0.5),
        'c_q_norm_g': gain((N_C, C_QK)),
        'c_k_norm_g': gain((N_C, C_QK)),
        'c_w_out': nrm((N_C, C_WIDTH, D_MODEL), out_scale * C_WIDTH ** -0.5),
    }


def reference(x, a_norm_g, a_w_in, a_q_norm_g, a_k_norm_g, a_lam_q1, a_lam_k1, a_lam_q2,
              a_lam_k2, a_sub_norm_g, a_w_out, b_norm_g, b_w_in, b_b_in, b_conv_w, b_conv_b,
              b_ln_g, b_ln_b, b_w_out, c_norm_g, c_w_in, c_cq_norm_g, c_w_uq, c_ckv_norm_g,
              c_w_ukv, c_q_norm_g, c_k_norm_g, c_w_out):
    for i in range(DEPTH):
        kind, j = i % N_MIXERS, i // N_MIXERS
        if kind == 0:
            lambda_init = 0.8 - 0.6 * math.exp(-0.3 * i)
            h = _rms_norm(x, a_norm_g[j])
            x = x + _diff_attention(h, a_w_in[j], a_q_norm_g[j], a_k_norm_g[j], a_lam_q1[j],
                                    a_lam_k1[j], a_lam_q2[j], a_lam_k2[j], a_sub_norm_g[j],
                                    a_w_out[j], lambda_init)
        elif kind == 1:
            h = _rms_norm(x, b_norm_g[j])
            x = x + _conformer_conv(h, b_w_in[j], b_b_in[j], b_conv_w[j], b_conv_b[j],
                                    b_ln_g[j], b_ln_b[j], b_w_out[j])
        else:
            h = _rms_norm(x, c_norm_g[j])
            x = x + _mla(h, c_w_in[j], c_cq_norm_g[j], c_w_uq[j], c_ckv_norm_g[j], c_w_ukv[j],
                         c_q_norm_g[j], c_k_norm_g[j], c_w_out[j])
    return x
```

```python
import functools
import math

import jax
import jax.numpy as jnp
from jax import lax
from jax.experimental import pallas as pl
from jax.experimental.pallas import tpu as pltpu

F32 = jnp.float32
BF16 = jnp.bfloat16

NORM_EPS = 1e-6
ROPE_THETA = 10000.0
CHUNK = 64
CHUNK_SHIFT = 6
N_MIXERS = 3

A_HEADS = 8
A_QK_DIM = 64
A_V_DIM = 128
C_HEADS = 8
C_NOPE = 128
C_ROPE = 64
C_QK = C_NOPE + C_ROPE
C_V = 128
C_Q_LORA = 256
C_KV_LORA = 128
C_HEAD_PAD = 256
B_CONV = 31
CONV_HALO = 32

LANES = 128
ATTN_TILE = 256
ROW_TILE = 1024
PREP_TILE = 256
CONV_TILE = 256
VMEM_LIMIT = 48 * 1024 * 1024
NEG_BIG = -1e30


def _cparams(sem):
    return pltpu.CompilerParams(dimension_semantics=sem, vmem_limit_bytes=VMEM_LIMIT)


def _sigmoid(x):
    return 1.0 / (1.0 + jnp.exp(-x))


def _inproj_kernel(x_ref, g_ref, w_ref, b_ref, o_ref, h_scr):
    @pl.when(pl.program_id(1) == 0)
    def _():
        x = x_ref[...]
        ms = jnp.mean(x * x, axis=-1, keepdims=True)
        h_scr[...] = (x * lax.rsqrt(ms + NORM_EPS) * g_ref[...]).astype(BF16)

    acc = jnp.dot(h_scr[...], w_ref[...], preferred_element_type=F32)
    o_ref[...] = (acc + b_ref[...]).astype(o_ref.dtype)


def _inproj(x2d, g, w, bias, tn):
    n_tok, d = x2d.shape
    n = w.shape[1]
    tm = min(ROW_TILE, n_tok)
    return pl.pallas_call(
        _inproj_kernel,
        grid=(n_tok // tm, n // tn),
        in_specs=[
            pl.BlockSpec((tm, d), lambda i, j: (i, 0)),
            pl.BlockSpec((1, d), lambda i, j: (0, 0)),
            pl.BlockSpec((d, tn), lambda i, j: (0, j)),
            pl.BlockSpec((1, tn), lambda i, j: (0, j)),
        ],
        out_specs=pl.BlockSpec((tm, tn), lambda i, j: (i, j)),
        out_shape=jax.ShapeDtypeStruct((n_tok, n), BF16),
        scratch_shapes=[pltpu.VMEM((tm, d), BF16)],
        compiler_params=_cparams(("parallel", "arbitrary")),
        name="inproj",
    )(x2d, g.reshape(1, d), w, bias.reshape(1, n))


def _outproj_kernel(y_ref, w_ref, x_ref, o_ref):
    o_ref[...] = x_ref[...] + jnp.dot(y_ref[...], w_ref[...], preferred_element_type=F32)


def _outproj(y2d, w, x2d):
    n_tok, k = y2d.shape
    d = w.shape[1]
    tm = min(ROW_TILE, n_tok)
    return pl.pallas_call(
        _outproj_kernel,
        grid=(n_tok // tm,),
        in_specs=[
            pl.BlockSpec((tm, k), lambda i: (i, 0)),
            pl.BlockSpec((k, d), lambda i: (0, 0)),
            pl.BlockSpec((tm, d), lambda i: (i, 0)),
        ],
        out_specs=pl.BlockSpec((tm, d), lambda i: (i, 0)),
        out_shape=jax.ShapeDtypeStruct((n_tok, d), F32),
        compiler_params=_cparams(("parallel",)),
        name="outproj",
    )(y2d, w, x2d)


def _rope_tables(seq):
    half = A_QK_DIM // 2
    inv = 1.0 / (ROPE_THETA ** (jnp.arange(0, A_QK_DIM, 2, dtype=F32) / A_QK_DIM))
    ang = jnp.arange(seq, dtype=F32)[:, None] * inv[None, :]
    cos, sin = jnp.cos(ang), jnp.sin(ang)
    cos_t = jnp.tile(cos, (1, LANES // half))
    sin_t = jnp.tile(jnp.concatenate([-sin, sin], axis=1), (1, LANES // A_QK_DIM))
    return cos_t, sin_t


def _rope64(x, cos_t, sin_t):
    lane = lax.broadcasted_iota(jnp.int32, x.shape, 1)
    first_half = (lane & (A_QK_DIM - 1)) < (A_QK_DIM // 2)
    partner = jnp.where(first_half,
                        pltpu.roll(x, LANES - A_QK_DIM // 2, 1),
                        pltpu.roll(x, A_QK_DIM // 2, 1))
    return x * cos_t + partner * sin_t


def _group64_mean_sq(x):
    lane = lax.broadcasted_iota(jnp.int32, x.shape, 1)
    lo = lane < A_QK_DIM
    sq = x * x
    s_all = jnp.sum(sq, axis=-1, keepdims=True)
    s_lo = jnp.sum(jnp.where(lo, sq, 0.0), axis=-1, keepdims=True)
    return jnp.where(lo, s_lo, s_all - s_lo) * (1.0 / A_QK_DIM)


def _a_prep_kernel(q_ref, k_ref, v_ref, cos_ref, sin_ref, qg_ref, kg_ref, qt_ref, ko_ref, vt_ref):
    cos_t, sin_t = cos_ref[...], sin_ref[...]

    def norm_rope(x, g):
        xn = x * lax.rsqrt(_group64_mean_sq(x) + NORM_EPS) * g
        return _rope64(xn, cos_t, sin_t)

    q = norm_rope(q_ref[0].astype(F32), qg_ref[...]) * (A_QK_DIM ** -0.5)
    qt_ref[0] = q.T.astype(BF16)
    ko_ref[0] = norm_rope(k_ref[0].astype(F32), kg_ref[...]).astype(BF16)
    vt_ref[0, 0, 0] = v_ref[0].astype(F32).T.astype(BF16)


def _a_prep(proj, cos_t, sin_t, q_g, k_g):
    bsz, seq, _ = proj.shape
    tp = min(PREP_TILE, seq)
    nh = A_HEADS
    hd = 2 * A_QK_DIM
    g2 = lambda g: jnp.tile(g, 2).reshape(1, hd)
    tok_spec = lambda off: pl.BlockSpec((1, tp, hd), lambda b, i, h: (b, i, off + h))
    tab_spec = pl.BlockSpec((tp, LANES), lambda b, i, h: (i, 0))
    par_spec = pl.BlockSpec((1, hd), lambda b, i, h: (0, 0))
    return pl.pallas_call(
        _a_prep_kernel,
        grid=(bsz, seq // tp, nh),
        in_specs=[tok_spec(0), tok_spec(nh), tok_spec(2 * nh), tab_spec, tab_spec, par_spec, par_spec],
        out_specs=[
            pl.BlockSpec((1, hd, tp), lambda b, i, h: (b, h, i)),
            pl.BlockSpec((1, tp, hd), lambda b, i, h: (b, i, h)),
            pl.BlockSpec((1, 1, 1, A_V_DIM, tp), lambda b, i, h: (b, h, i, 0, 0)),
        ],
        out_shape=[
            jax.ShapeDtypeStruct((bsz, nh * hd, seq), BF16),
            jax.ShapeDtypeStruct((bsz, seq, nh * hd), BF16),
            jax.ShapeDtypeStruct((bsz, nh, seq // tp, A_V_DIM, tp), BF16),
        ],
        compiler_params=_cparams(("parallel", "parallel", "parallel")),
        name="a_prep",
    )(proj, proj, proj, cos_t, sin_t, g2(q_g), g2(k_g))


def _attn_kernel(*refs, n_maps, lambda_init):
    if n_maps == 2:
        (qt_ref, k_ref, vt_ref, gate_ref, lq1_ref, lk1_ref, lq2_ref, lk2_ref, subg_ref,
         o_ref, acc_scr, m_scr, l_scr) = refs
    else:
        qt_ref, k_ref, vt_ref, gate_ref, o_ref, acc_scr, m_scr, l_scr = refs
    j = pl.program_id(2)
    tq = qt_ref.shape[2]
    tk = vt_ref.shape[4]

    qt = qt_ref[0]
    if n_maps == 2:
        row = lax.broadcasted_iota(jnp.int32, qt.shape, 0)
        zero = jnp.zeros_like(qt)
        q_maps = [jnp.where(row < A_QK_DIM, qt, zero), jnp.where(row >= A_QK_DIM, qt, zero)]
    else:
        q_maps = [qt]

    m_scr[...] = jnp.full(m_scr.shape, NEG_BIG, F32)
    l_scr[...] = jnp.zeros(l_scr.shape, F32)
    acc_scr[...] = jnp.zeros(acc_scr.shape, F32)

    def step(i, masked):
        start = pl.multiple_of(i * tk, tk)
        kb = k_ref[0, pl.ds(start, tk), :]
        vb = vt_ref[0, 0, i]
        for mi in range(n_maps):
            s = jnp.dot(kb, q_maps[mi], preferred_element_type=F32)
            if masked:
                kk = lax.broadcasted_iota(jnp.int32, s.shape, 0)
                qq = lax.broadcasted_iota(jnp.int32, s.shape, 1)
                s = jnp.where((kk >> CHUNK_SHIFT) <= (qq >> CHUNK_SHIFT), s, NEG_BIG)
            m_old = m_scr[mi]
            m_new = jnp.maximum(m_old, jnp.max(s, axis=0, keepdims=True))
            alpha = jnp.exp(m_old - m_new)
            p = jnp.exp(s - m_new)
            l_scr[mi] = alpha * l_scr[mi] + jnp.sum(p, axis=0, keepdims=True)
            acc_scr[mi] = alpha * acc_scr[mi] + jnp.dot(vb, p.astype(BF16), preferred_element_type=F32)
            m_scr[mi] = m_new

    def body(i, carry):
        step(i, False)
        return carry

    lax.fori_loop(0, j, body, 0)
    step(j, True)

    o_t = acc_scr[0] / l_scr[0]
    if n_maps == 2:
        f = lambda r: jnp.sum(r[...], axis=-1, keepdims=True)
        lam = (jnp.exp(f(lq1_ref[...] * lk1_ref[...])) - jnp.exp(f(lq2_ref[...] * lk2_ref[...]))
               + lambda_init)
        o_t = o_t - lam * (acc_scr[1] / l_scr[1])
    o = o_t.T
    if n_maps == 2:
        ms = jnp.mean(o * o, axis=-1, keepdims=True)
        o = o * lax.rsqrt(ms + NORM_EPS) * subg_ref[...] * (1.0 - lambda_init)
    g = gate_ref[0].astype(F32)
    o_ref[0] = (o * (g * _sigmoid(g))).astype(o_ref.dtype)


def _attention(qt, k, vt, gate_src, gate_blk0, n_maps, lam_params=(), sub_g=None, lambda_init=0.0):
    bsz, nh, nkv, dv, tk = vt.shape
    seq = nkv * tk
    dk = qt.shape[1] // nh
    tq = tk
    in_specs = [
        pl.BlockSpec((1, dk, tq), lambda b, h, j: (b, h, j)),
        pl.BlockSpec((1, seq, dk), lambda b, h, j: (b, 0, h)),
        pl.BlockSpec((1, 1, nkv, dv, tk), lambda b, h, j: (b, h, 0, 0, 0)),
        pl.BlockSpec((1, tq, dv), lambda b, h, j: (b, j, gate_blk0 + h)),
    ]
    args = [qt, k, vt, gate_src]
    if n_maps == 2:
        for p in lam_params:
            in_specs.append(pl.BlockSpec((1, A_QK_DIM), lambda b, h, j: (0, 0)))
            args.append(p.reshape(1, A_QK_DIM))
        in_specs.append(pl.BlockSpec((1, dv), lambda b, h, j: (0, 0)))
        args.append(sub_g.reshape(1, dv))
    return pl.pallas_call(
        functools.partial(_attn_kernel, n_maps=n_maps, lambda_init=lambda_init),
        grid=(bsz, nh, seq // tq),
        in_specs=in_specs,
        out_specs=pl.BlockSpec((1, tq, dv), lambda b, h, j: (b, j, h)),
        out_shape=jax.ShapeDtypeStruct((bsz, seq, nh * dv), BF16),
        scratch_shapes=[
            pltpu.VMEM((n_maps, dv, tq), F32),
            pltpu.VMEM((n_maps, 1, tq), F32),
            pltpu.VMEM((n_maps, 1, tq), F32),
        ],
        compiler_params=_cparams(("parallel", "parallel", "arbitrary")),
        name="attn%d" % n_maps,
    )(*args)


def _conv_kernel(a_ref, b_ref, gate_ref, w_ref, cb_ref, lng_ref, lnb_ref, o_ref, u_scr):
    tc = a_ref.shape[1]
    j = pl.program_id(1)

    @pl.when(j == 0)
    def _():
        u_scr[0:CONV_HALO, :] = jnp.zeros((CONV_HALO, u_scr.shape[1]), F32)

    @pl.when(j > 0)
    def _():
        u_scr[0:CONV_HALO, :] = u_scr[tc:tc + CONV_HALO, :]

    u_scr[CONV_HALO:CONV_HALO + tc, :] = a_ref[0].astype(F32) * _sigmoid(b_ref[0].astype(F32))
    acc = jnp.broadcast_to(cb_ref[...], (tc, u_scr.shape[1]))
    base = CONV_HALO - (B_CONV - 1)
    for t in range(B_CONV):
        acc = acc + u_scr[base + t:base + t + tc, :] * w_ref[t:t + 1, :]
    mu = jnp.mean(acc, axis=-1, keepdims=True)
    xc = acc - mu
    var = jnp.mean(xc * xc, axis=-1, keepdims=True)
    y = xc * lax.rsqrt(var + NORM_EPS) * lng_ref[...] + lnb_ref[...]
    y = y * _sigmoid(y)
    g = gate_ref[0].astype(F32)
    o_ref[0] = (y * (g * _sigmoid(g))).astype(o_ref.dtype)


def _conv_module(proj, conv_w, conv_b, ln_g, ln_b):
    bsz, seq, w3 = proj.shape
    w = w3 // 3
    tc = min(CONV_TILE, seq)
    blk = lambda c: pl.BlockSpec((1, tc, w), lambda b, j: (b, j, c))
    par = pl.BlockSpec((1, w), lambda b, j: (0, 0))
    return pl.pallas_call(
        _conv_kernel,
        grid=(bsz, seq // tc),
        in_specs=[blk(0), blk(1), blk(2), pl.BlockSpec((B_CONV, w), lambda b, j: (0, 0)), par, par, par],
        out_specs=pl.BlockSpec((1, tc, w), lambda b, j: (b, j, 0)),
        out_shape=jax.ShapeDtypeStruct((bsz, seq, w), BF16),
        scratch_shapes=[pltpu.VMEM((tc + CONV_HALO, w), F32)],
        compiler_params=_cparams(("parallel", "arbitrary")),
        name="conv",
    )(proj, proj, proj, conv_w, conv_b.reshape(1, w), ln_g.reshape(1, w), ln_b.reshape(1, w))


def _c_prep_kernel(p_ref, cos_ref, sin_ref, cqg_ref, ckvg_ref, wuq_ref, wukv_ref,
                   qgn_ref, qgr_ref, kgn_ref, kgr_ref, qt_ref, ko_ref, vt_ref):
    cos_t, sin_t = cos_ref[...], sin_ref[...]
    p = p_ref[0].astype(F32)

    def rms(x, g, width):
        ms = jnp.sum(x * x, axis=-1, keepdims=True) * (1.0 / width)
        return x * lax.rsqrt(ms + NORM_EPS) * g

    c_q = rms(p[:, :C_Q_LORA], cqg_ref[...], C_Q_LORA).astype(BF16)
    c_kv = rms(p[:, C_Q_LORA:C_Q_LORA + C_KV_LORA], ckvg_ref[...], C_KV_LORA).astype(BF16)
    q_all = jnp.dot(c_q, wuq_ref[...], preferred_element_type=F32)
    kv_all = jnp.dot(c_kv, wukv_ref[...], preferred_element_type=F32)
    k_rope = _rope64(rms(p[:, C_Q_LORA + C_KV_LORA:], kgr_ref[...], C_ROPE), cos_t, sin_t).astype(BF16)
    scale = C_QK ** -0.5
    for h in range(C_HEADS):
        c0 = h * C_HEAD_PAD
        qn = rms(q_all[:, c0:c0 + C_NOPE], qgn_ref[...], C_NOPE) * scale
        qr = _rope64(rms(q_all[:, c0 + C_NOPE:c0 + C_HEAD_PAD], qgr_ref[...], C_ROPE), cos_t, sin_t) * scale
        qt_ref[0, c0:c0 + C_NOPE, :] = qn.T.astype(BF16)
        qt_ref[0, c0 + C_NOPE:c0 + C_HEAD_PAD, :] = qr.T.astype(BF16)
        ko_ref[0, :, c0:c0 + C_NOPE] = rms(kv_all[:, c0:c0 + C_NOPE], kgn_ref[...], C_NOPE).astype(BF16)
        ko_ref[0, :, c0 + C_NOPE:c0 + C_HEAD_PAD] = k_rope
        vt_ref[0, h, 0] = kv_all[:, c0 + C_NOPE:c0 + C_HEAD_PAD].T.astype(BF16)


def _c_prep(proj, cos_t, sin_t, cq_g, ckv_g, w_uq, w_ukv, q_g, k_g):
    bsz, seq, _ = proj.shape
    tp = min(PREP_TILE, seq)
    nh = C_HEADS
    lead = C_Q_LORA + C_KV_LORA + LANES
    pad_rope = lambda g: jnp.concatenate([g, jnp.zeros((LANES - C_ROPE,), F32)]).reshape(1, LANES)
    wq = w_uq.reshape(C_Q_LORA, nh, C_QK)
    wq = jnp.concatenate([wq, jnp.zeros((C_Q_LORA, nh, C_HEAD_PAD - C_QK), w_uq.dtype)], axis=2)
    wq = wq.reshape(C_Q_LORA, nh * C_HEAD_PAD).astype(BF16)
    full = lambda shape: pl.BlockSpec(shape, lambda b, i: (0,) * len(shape))
    return pl.pallas_call(
        _c_prep_kernel,
        grid=(bsz, seq // tp),
        in_specs=[
            pl.BlockSpec((1, tp, lead), lambda b, i: (b, i, 0)),
            pl.BlockSpec((tp, LANES), lambda b, i: (i, 0)),
            pl.BlockSpec((tp, LANES), lambda b, i: (i, 0)),
            full((1, C_Q_LORA)), full((1, C_KV_LORA)),
            full((C_Q_LORA, nh * C_HEAD_PAD)), full((C_KV_LORA, nh * C_HEAD_PAD)),
            full((1, C_NOPE)), full((1, LANES)), full((1, C_NOPE)), full((1, LANES)),
        ],
        out_specs=[
            pl.BlockSpec((1, nh * C_HEAD_PAD, tp), lambda b, i: (b, 0, i)),
            pl.BlockSpec((1, tp, nh * C_HEAD_PAD), lambda b, i: (b, i, 0)),
            pl.BlockSpec((1, nh, 1, C_V, tp), lambda b, i: (b, 0, i, 0, 0)),
        ],
        out_shape=[
            jax.ShapeDtypeStruct((bsz, nh * C_HEAD_PAD, seq), BF16),
            jax.ShapeDtypeStruct((bsz, seq, nh * C_HEAD_PAD), BF16),
            jax.ShapeDtypeStruct((bsz, nh, seq // tp, C_V, tp), BF16),
        ],
        compiler_params=_cparams(("parallel", "parallel")),
        name="c_prep",
    )(proj, cos_t, sin_t, cq_g.reshape(1, -1), ckv_g.reshape(1, -1), wq, w_ukv.astype(BF16),
      q_g[:C_NOPE].reshape(1, -1), pad_rope(q_g[C_NOPE:]), k_g[:C_NOPE].reshape(1, -1), pad_rope(k_g[C_NOPE:]))


def _layer_a(x, tabs, norm_g, w_in, q_g, k_g, lq1, lk1, lq2, lk2, sub_g, w_out, lambda_init):
    bsz, seq, d = x.shape
    x2d = x.reshape(bsz * seq, d)
    n_in = w_in.shape[1]
    proj = _inproj(x2d, norm_g, w_in.astype(BF16), jnp.zeros((n_in,), F32), 1024).reshape(bsz, seq, n_in)
    qt, k, vt = _a_prep(proj, tabs[0], tabs[1], q_g, k_g)
    gate_blk0 = (n_in - A_HEADS * A_V_DIM) // LANES
    y = _attention(qt, k, vt, proj, gate_blk0, 2, (lq1, lk1, lq2, lk2), sub_g, lambda_init)
    return _outproj(y.reshape(bsz * seq, -1), w_out.astype(BF16), x2d).reshape(bsz, seq, d)


def _layer_b(x, norm_g, w_in, b_in, conv_w, conv_b, ln_g, ln_b, w_out):
    bsz, seq, d = x.shape
    x2d = x.reshape(bsz * seq, d)
    n_in = w_in.shape[1]
    proj = _inproj(x2d, norm_g, w_in.astype(BF16), b_in, 1024).reshape(bsz, seq, n_in)
    y = _conv_module(proj, conv_w, conv_b, ln_g, ln_b)
    return _outproj(y.reshape(bsz * seq, -1), w_out.astype(BF16), x2d).reshape(bsz, seq, d)


def _layer_c(x, tabs, norm_g, w_in, cq_g, w_uq, ckv_g, w_ukv, q_g, k_g, w_out):
    bsz, seq, d = x.shape
    x2d = x.reshape(bsz * seq, d)
    n_lead = C_Q_LORA + C_KV_LORA + C_ROPE
    w = jnp.concatenate([w_in[:, :n_lead], jnp.zeros((d, LANES - C_ROPE), w_in.dtype), w_in[:, n_lead:]], axis=1)
    n_in = w.shape[1]
    proj = _inproj(x2d, norm_g, w.astype(BF16), jnp.zeros((n_in,), F32), 512).reshape(bsz, seq, n_in)
    qt, k, vt = _c_prep(proj, tabs[0], tabs[1], cq_g, ckv_g, w_uq, w_ukv, q_g, k_g)
    gate_blk0 = (n_lead + LANES - C_ROPE) // LANES
    y = _attention(qt, k, vt, proj, gate_blk0, 1)
    return _outproj(y.reshape(bsz * seq, -1), w_out.astype(BF16), x2d).reshape(bsz, seq, d)


def kernel(x, a_norm_g, a_w_in, a_q_norm_g, a_k_norm_g, a_lam_q1, a_lam_k1, a_lam_q2, a_lam_k2, a_sub_norm_g, a_w_out, b_norm_g, b_w_in, b_b_in, b_conv_w, b_conv_b, b_ln_g, b_ln_b, b_w_out, c_norm_g, c_w_in, c_cq_norm_g, c_w_uq, c_ckv_norm_g, c_w_ukv, c_q_norm_g, c_k_norm_g, c_w_out):
    depth = a_norm_g.shape[0] + b_norm_g.shape[0] + c_norm_g.shape[0]
    tabs = _rope_tables(x.shape[1])
    for i in range(depth):
        kind, j = i % N_MIXERS, i // N_MIXERS
        if kind == 0:
            lambda_init = 0.8 - 0.6 * math.exp(-0.3 * i)
            x = _layer_a(x, tabs, a_norm_g[j], a_w_in[j], a_q_norm_g[j], a_k_norm_g[j], a_lam_q1[j],
                         a_lam_k1[j], a_lam_q2[j], a_lam_k2[j], a_sub_norm_g[j], a_w_out[j], lambda_init)
        elif kind == 1:
            x = _layer_b(x, b_norm_g[j], b_w_in[j], b_b_in[j], b_conv_w[j], b_conv_b[j], b_ln_g[j],
                         b_ln_b[j], b_w_out[j])
        else:
            x = _layer_c(x, tabs, c_norm_g[j], c_w_in[j], c_cq_norm_g[j], c_w_uq[j], c_ckv_norm_g[j],
                         c_w_ukv[j], c_q_norm_g[j], c_k_norm_g[j], c_w_out[j])
    return x
```

```python
import functools
import math

import numpy as np
import jax
import jax.numpy as jnp
from jax import lax
from jax.experimental import pallas as pl
from jax.experimental.pallas import tpu as pltpu

F32 = jnp.float32
BF16 = jnp.bfloat16

NORM_EPS = 1e-6
ROPE_THETA = 10000.0
CHUNK_SHIFT = 6
N_MIXERS = 3

A_HEADS = 8
A_QK_DIM = 64
A_V_DIM = 128
C_HEADS = 8
C_NOPE = 128
C_ROPE = 64
C_QK = C_NOPE + C_ROPE
C_V = 128
C_Q_LORA = 256
C_KV_LORA = 128
C_HEAD_PAD = 256
B_CONV = 31
CONV_HALO = 32

LANES = 128
KEY_TILE = 256
ATTN_COLS = 512
ROW_TILE = 1024
CONV_TILE = 256
VMEM_LIMIT = 48 * 1024 * 1024
NEG_BIG = -1e30
LOG2E = math.log2(math.e)

BIAS_MASK = 3
FIRST = 4
LAST = 8


def _cparams(sem):
    return pltpu.CompilerParams(dimension_semantics=sem, vmem_limit_bytes=VMEM_LIMIT)


def _sigmoid(x):
    return 1.0 / (1.0 + jnp.exp(-x))


def _inproj_kernel(x_ref, g_ref, w_ref, b_ref, o_ref, h_scr):
    @pl.when(pl.program_id(1) == 0)
    def _():
        x = x_ref[...]
        ms = jnp.mean(x * x, axis=-1, keepdims=True)
        h_scr[...] = (x * lax.rsqrt(ms + NORM_EPS) * g_ref[...]).astype(BF16)

    acc = jnp.dot(h_scr[...], w_ref[...], preferred_element_type=F32)
    o_ref[...] = (acc + b_ref[...]).astype(o_ref.dtype)


def _inproj(x2d, g, w, bias, tn):
    n_tok, d = x2d.shape
    n = w.shape[1]
    tm = min(ROW_TILE, n_tok)
    return pl.pallas_call(
        _inproj_kernel,
        grid=(n_tok // tm, n // tn),
        in_specs=[
            pl.BlockSpec((tm, d), lambda i, j: (i, 0)),
            pl.BlockSpec((1, d), lambda i, j: (0, 0)),
            pl.BlockSpec((d, tn), lambda i, j: (0, j)),
            pl.BlockSpec((1, tn), lambda i, j: (0, j)),
        ],
        out_specs=pl.BlockSpec((tm, tn), lambda i, j: (i, j)),
        out_shape=jax.ShapeDtypeStruct((n_tok, n), BF16),
        scratch_shapes=[pltpu.VMEM((tm, d), BF16)],
        compiler_params=_cparams(("parallel", "arbitrary")),
        name="inproj",
    )(x2d, g.reshape(1, d), w, bias.reshape(1, n))


def _outproj_kernel(y_ref, w_ref, x_ref, o_ref):
    o_ref[...] = x_ref[...] + jnp.dot(y_ref[...], w_ref[...], preferred_element_type=F32)


def _outproj(y2d, w, x2d):
    n_tok, k = y2d.shape
    d = w.shape[1]
    tm = min(ROW_TILE, n_tok)
    return pl.pallas_call(
        _outproj_kernel,
        grid=(n_tok // tm,),
        in_specs=[
            pl.BlockSpec((tm, k), lambda i: (i, 0)),
            pl.BlockSpec((k, d), lambda i: (0, 0)),
            pl.BlockSpec((tm, d), lambda i: (i, 0)),
        ],
        out_specs=pl.BlockSpec((tm, d), lambda i: (i, 0)),
        out_shape=jax.ShapeDtypeStruct((n_tok, d), F32),
        compiler_params=_cparams(("parallel",)),
        name="outproj",
    )(y2d, w, x2d)


def _rope_tables(seq):
    half = A_QK_DIM // 2
    inv = 1.0 / (ROPE_THETA ** (jnp.arange(0, A_QK_DIM, 2, dtype=F32) / A_QK_DIM))
    ang = jnp.arange(seq, dtype=F32)[:, None] * inv[None, :]
    cos, sin = jnp.cos(ang), jnp.sin(ang)
    cos_t = jnp.tile(cos, (1, LANES // half))
    sin_t = jnp.tile(jnp.concatenate([-sin, sin], axis=1), (1, LANES // A_QK_DIM))
    return cos_t, sin_t


def _rope64(x, cos_t, sin_t):
    lane = lax.broadcasted_iota(jnp.int32, x.shape, 1)
    first_half = (lane & (A_QK_DIM - 1)) < (A_QK_DIM // 2)
    partner = jnp.where(first_half,
                        pltpu.roll(x, LANES - A_QK_DIM // 2, 1),
                        pltpu.roll(x, A_QK_DIM // 2, 1))
    return x * cos_t + partner * sin_t


def _group64_mean_sq(x):
    lane = lax.broadcasted_iota(jnp.int32, x.shape, 1)
    lo = lane < A_QK_DIM
    sq = x * x
    s_all = jnp.sum(sq, axis=-1, keepdims=True)
    s_lo = jnp.sum(jnp.where(lo, sq, 0.0), axis=-1, keepdims=True)
    return jnp.where(lo, s_lo, s_all - s_lo) * (1.0 / A_QK_DIM)


def _a_prep_kernel(q_ref, k_ref, v_ref, cos_ref, sin_ref, qg_ref, kg_ref, qt_ref, ko_ref, vt_ref):
    cos_t, sin_t = cos_ref[...], sin_ref[...]

    def norm_rope(x, g):
        xn = x * lax.rsqrt(_group64_mean_sq(x) + NORM_EPS) * g
        return _rope64(xn, cos_t, sin_t)

    q = norm_rope(q_ref[0].astype(F32), qg_ref[...]) * (A_QK_DIM ** -0.5 * LOG2E)
    q_t = q.T.astype(BF16)
    row = lax.broadcasted_iota(jnp.int32, q_t.shape, 0)
    zero = jnp.zeros_like(q_t)
    qt_ref[0, 0, 0] = jnp.concatenate(
        [jnp.where(row < A_QK_DIM, q_t, zero), jnp.where(row >= A_QK_DIM, q_t, zero)], axis=1)
    ko_ref[0] = norm_rope(k_ref[0].astype(F32), kg_ref[...]).astype(BF16)
    vt_ref[0, 0, 0] = v_ref[0].astype(F32).T.astype(BF16)


def _a_prep(proj, cos_t, sin_t, q_g, k_g):
    bsz, seq, _ = proj.shape
    tp = KEY_TILE
    nh = A_HEADS
    hd = 2 * A_QK_DIM
    g2 = lambda g: jnp.tile(g, 2).reshape(1, hd)
    tok_spec = lambda off: pl.BlockSpec((1, tp, hd), lambda b, i, h: (b, i, off + h))
    tab_spec = pl.BlockSpec((tp, LANES), lambda b, i, h: (i, 0))
    par_spec = pl.BlockSpec((1, hd), lambda b, i, h: (0, 0))
    return pl.pallas_call(
        _a_prep_kernel,
        grid=(bsz, seq // tp, nh),
        in_specs=[tok_spec(0), tok_spec(nh), tok_spec(2 * nh), tab_spec, tab_spec, par_spec, par_spec],
        out_specs=[
            pl.BlockSpec((1, 1, 1, hd, 2 * tp), lambda b, i, h: (b, h, i, 0, 0)),
            pl.BlockSpec((1, tp, hd), lambda b, i, h: (b, i, h)),
            pl.BlockSpec((1, 1, 1, A_V_DIM, tp), lambda b, i, h: (b, h, i, 0, 0)),
        ],
        out_shape=[
            jax.ShapeDtypeStruct((bsz, nh, seq // tp, hd, 2 * tp), BF16),
            jax.ShapeDtypeStruct((bsz, seq, nh * hd), BF16),
            jax.ShapeDtypeStruct((bsz, nh, seq // tp, A_V_DIM, tp), BF16),
        ],
        compiler_params=_cparams(("parallel", "parallel", "parallel")),
        name="a_prep",
    )(proj, proj, proj, cos_t, sin_t, g2(q_g), g2(k_g))


def _pair_schedule(n_qtiles, tq_tok):
    per_q = tq_tok // KEY_TILE
    slots = []
    for j in range(n_qtiles):
        n_k = (j + 1) * per_q
        for i in range(n_k):
            diag = i - j * per_q
            flags = (diag + 1 if diag >= 0 else 0) | (FIRST if i == 0 else 0) | (LAST if i == n_k - 1 else 0)
            slots.append((j, i, flags))
    assert all(not (a[2] & LAST and b[2] & LAST) for a, b in zip(slots, slots[1:]))
    n_iters = len(slots) + 2
    n_iters += n_iters % 2
    dummy = (0, 0, FIRST)
    slots = [dummy, dummy] + slots
    slots += [dummy] * (n_iters + 2 - len(slots))
    tab = np.asarray(slots, np.int32)
    return tab[:, 0], tab[:, 1], tab[:, 2], n_iters


def _mask_bias(tq_tok, n_maps):
    per_q = tq_tok // KEY_TILE
    kk = np.arange(KEY_TILE)[:, None]
    q_tok = (np.arange(tq_tok * n_maps) % tq_tok)[None, :]
    bias = np.zeros((per_q + 1, KEY_TILE, tq_tok * n_maps), np.float32)
    for f in range(per_q):
        masked = ((kk + f * KEY_TILE) >> CHUNK_SHIFT) > (q_tok >> CHUNK_SHIFT)
        bias[f + 1] = np.where(masked, NEG_BIG, 0.0)
    return bias


def _attn_kernel(ej_ref, ei_ref, ef_ref, qt_ref, k_ref, vt_ref, gate_ref, bias_ref, *rest,
                 n_maps, lambda_init, n_iters):
    if n_maps == 2:
        lq1_ref, lk1_ref, lq2_ref, lk2_ref, subg_ref, o_ref, *scr = rest
    else:
        o_ref, *scr = rest
    s_a, s_b, cm_a, cm_b, p_a, p_b, al_a, al_b, acc_scr, lfin_scr = scr
    s_buf, cm_buf, p_buf, al_buf = (s_a, s_b), (cm_a, cm_b), (p_a, p_b), (al_a, al_b)
    tq_tok = ATTN_COLS // n_maps

    for r in scr:
        r[...] = jnp.zeros(r.shape, r.dtype)

    def finalize(j):
        o_cat = acc_scr[...] / lfin_scr[...]
        if n_maps == 2:
            f = lambda r: jnp.sum(r, axis=-1, keepdims=True)
            lam = (jnp.exp(f(lq1_ref[...] * lk1_ref[...])) - jnp.exp(f(lq2_ref[...] * lk2_ref[...]))
                   + lambda_init)
            o = (o_cat[:, :tq_tok] - lam * o_cat[:, tq_tok:]).T
            ms = jnp.mean(o * o, axis=-1, keepdims=True)
            o = o * lax.rsqrt(ms + NORM_EPS) * subg_ref[...] * (1.0 - lambda_init)
        else:
            o = o_cat.T
        rows = pl.ds(pl.multiple_of(j * tq_tok, tq_tok), tq_tok)
        g = gate_ref[0, rows, :].astype(F32)
        o_ref[0, rows, :] = (o * (g * _sigmoid(g))).astype(o_ref.dtype)

    def iteration(t, par, m, l):
        j3, i3, f3 = ej_ref[t], ei_ref[t], ef_ref[t]
        pv = jnp.dot(vt_ref[0, 0, i3], p_buf[1 - par][...], preferred_element_type=F32)
        acc_scr[...] = al_buf[1 - par][...] * acc_scr[...] + pv
        j1, i1, f1 = ej_ref[t + 2], ei_ref[t + 2], ef_ref[t + 2]
        kb = k_ref[0, pl.ds(pl.multiple_of(i1 * KEY_TILE, KEY_TILE), KEY_TILE), :]
        s = jnp.dot(kb, qt_ref[0, 0, j1], preferred_element_type=F32) + bias_ref[f1 & BIAS_MASK]
        s_buf[par][...] = s
        cm_buf[par][...] = jnp.max(s, axis=0, keepdims=True)
        f2 = ef_ref[t + 1]
        m_old = jnp.where((f2 & FIRST) != 0, NEG_BIG, m)
        m_new = jnp.maximum(m_old, cm_buf[1 - par][...])
        alpha = jnp.exp2(m_old - m_new)
        p = jnp.exp2(s_buf[1 - par][...] - m_new)
        l_new = alpha * l + jnp.sum(p, axis=0, keepdims=True)
        p_buf[par][...] = p.astype(BF16)
        al_buf[par][...] = alpha
        lfin_scr[...] = jnp.where((f2 & LAST) != 0, l_new, lfin_scr[...])

        @pl.when((f3 & LAST) != 0)
        def _():
            finalize(j3)

        return m_new, l_new

    def body(v, carry):
        m, l = carry
        m, l = iteration(2 * v, 0, m, l)
        return iteration(2 * v + 1, 1, m, l)

    stat0 = jnp.zeros((1, ATTN_COLS), F32)
    lax.fori_loop(0, n_iters // 2, body, (stat0, stat0))


def _attention(qt, k, vt, gate_src, gate_blk0, n_maps, lam_params=(), sub_g=None, lambda_init=0.0):
    bsz, nh, nq, dk, cols = qt.shape
    _, _, nkv, dv, tk = vt.shape
    assert cols == ATTN_COLS and tk == KEY_TILE
    seq = nkv * tk
    tq_tok = cols // n_maps
    ej, ei, ef, n_iters = _pair_schedule(nq, tq_tok)
    bias = _mask_bias(tq_tok, n_maps)
    in_specs = [
        pl.BlockSpec((1, 1, nq, dk, cols), lambda b, h, *_: (b, h, 0, 0, 0)),
        pl.BlockSpec((1, seq, dk), lambda b, h, *_: (b, 0, h)),
        pl.BlockSpec((1, 1, nkv, dv, tk), lambda b, h, *_: (b, h, 0, 0, 0)),
        pl.BlockSpec((1, seq, dv), lambda b, h, *_: (b, 0, gate_blk0 + h)),
        pl.BlockSpec(bias.shape, lambda b, h, *_: (0, 0, 0)),
    ]
    args = [qt, k, vt, gate_src, jnp.asarray(bias)]
    if n_maps == 2:
        for p in lam_params:
            in_specs.append(pl.BlockSpec((1, A_QK_DIM), lambda b, h, *_: (0, 0)))
            args.append(p.reshape(1, A_QK_DIM))
        in_specs.append(pl.BlockSpec((1, dv), lambda b, h, *_: (0, 0)))
        args.append(sub_g.reshape(1, dv))
    stat = pltpu.VMEM((1, cols), F32)
    return pl.pallas_call(
        functools.partial(_attn_kernel, n_maps=n_maps, lambda_init=lambda_init, n_iters=n_iters),
        grid_spec=pltpu.PrefetchScalarGridSpec(
            num_scalar_prefetch=3,
            grid=(bsz, nh),
            in_specs=in_specs,
            out_specs=pl.BlockSpec((1, seq, dv), lambda b, h, *_: (b, 0, h)),
            scratch_shapes=[
                pltpu.VMEM((tk, cols), F32), pltpu.VMEM((tk, cols), F32), stat, stat,
                pltpu.VMEM((tk, cols), BF16), pltpu.VMEM((tk, cols), BF16), stat, stat,
                pltpu.VMEM((dv, cols), F32), stat,
            ],
        ),
        out_shape=jax.ShapeDtypeStruct((bsz, seq, nh * dv), BF16),
        compiler_params=_cparams(("parallel", "parallel")),
        name="attn%d" % n_maps,
    )(jnp.asarray(ej), jnp.asarray(ei), jnp.asarray(ef), *args)


def _conv_kernel(a_ref, b_ref, gate_ref, w_ref, cb_ref, lng_ref, lnb_ref, o_ref, u_scr):
    tc = a_ref.shape[1]
    j = pl.program_id(1)

    @pl.when(j == 0)
    def _():
        u_scr[0:CONV_HALO, :] = jnp.zeros((CONV_HALO, u_scr.shape[1]), F32)

    @pl.when(j > 0)
    def _():
        u_scr[0:CONV_HALO, :] = u_scr[tc:tc + CONV_HALO, :]

    u_scr[CONV_HALO:CONV_HALO + tc, :] = a_ref[0].astype(F32) * _sigmoid(b_ref[0].astype(F32))
    acc = jnp.broadcast_to(cb_ref[...], (tc, u_scr.shape[1]))
    base = CONV_HALO - (B_CONV - 1)
    for t in range(B_CONV):
        acc = acc + u_scr[base + t:base + t + tc, :] * w_ref[t:t + 1, :]
    mu = jnp.mean(acc, axis=-1, keepdims=True)
    xc = acc - mu
    var = jnp.mean(xc * xc, axis=-1, keepdims=True)
    y = xc * lax.rsqrt(var + NORM_EPS) * lng_ref[...] + lnb_ref[...]
    y = y * _sigmoid(y)
    g = gate_ref[0].astype(F32)
    o_ref[0] = (y * (g * _sigmoid(g))).astype(o_ref.dtype)


def _conv_module(proj, conv_w, conv_b, ln_g, ln_b):
    bsz, seq, w3 = proj.shape
    w = w3 // 3
    tc = min(CONV_TILE, seq)
    blk = lambda c: pl.BlockSpec((1, tc, w), lambda b, j: (b, j, c))
    par = pl.BlockSpec((1, w), lambda b, j: (0, 0))
    return pl.pallas_call(
        _conv_kernel,
        grid=(bsz, seq // tc),
        in_specs=[blk(0), blk(1), blk(2), pl.BlockSpec((B_CONV, w), lambda b, j: (0, 0)), par, par, par],
        out_specs=pl.BlockSpec((1, tc, w), lambda b, j: (b, j, 0)),
        out_shape=jax.ShapeDtypeStruct((bsz, seq, w), BF16),
        scratch_shapes=[pltpu.VMEM((tc + CONV_HALO, w), F32)],
        compiler_params=_cparams(("parallel", "arbitrary")),
        name="conv",
    )(proj, proj, proj, conv_w, conv_b.reshape(1, w), ln_g.reshape(1, w), ln_b.reshape(1, w))


def _c_prep_kernel(p_ref, cos_ref, sin_ref, cqg_ref, ckvg_ref, wuq_ref, wukv_ref,
                   qgn_ref, qgr_ref, kgn_ref, kgr_ref, qt_ref, ko_ref, vt_ref):
    cos_t, sin_t = cos_ref[...], sin_ref[...]
    p = p_ref[0].astype(F32)

    def rms(x, g, width):
        ms = jnp.sum(x * x, axis=-1, keepdims=True) * (1.0 / width)
        return x * lax.rsqrt(ms + NORM_EPS) * g

    c_q = rms(p[:, :C_Q_LORA], cqg_ref[...], C_Q_LORA).astype(BF16)
    c_kv = rms(p[:, C_Q_LORA:C_Q_LORA + C_KV_LORA], ckvg_ref[...], C_KV_LORA).astype(BF16)
    q_all = jnp.dot(c_q, wuq_ref[...], preferred_element_type=F32)
    kv_all = jnp.dot(c_kv, wukv_ref[...], preferred_element_type=F32)
    k_rope = _rope64(rms(p[:, C_Q_LORA + C_KV_LORA:], kgr_ref[...], C_ROPE), cos_t, sin_t).astype(BF16)
    scale = C_QK ** -0.5 * LOG2E
    for h in range(C_HEADS):
        c0 = h * C_HEAD_PAD
        qn = rms(q_all[:, c0:c0 + C_NOPE], qgn_ref[...], C_NOPE) * scale
        qr = _rope64(rms(q_all[:, c0 + C_NOPE:c0 + C_HEAD_PAD], qgr_ref[...], C_ROPE), cos_t, sin_t) * scale
        qt_ref[0, h, 0, 0:C_NOPE, :] = qn.T.astype(BF16)
        qt_ref[0, h, 0, C_NOPE:C_HEAD_PAD, :] = qr.T.astype(BF16)
        ko_ref[0, :, c0:c0 + C_NOPE] = rms(kv_all[:, c0:c0 + C_NOPE], kgn_ref[...], C_NOPE).astype(BF16)
        ko_ref[0, :, c0 + C_NOPE:c0 + C_HEAD_PAD] = k_rope
        vt_ref[0, h, 0] = kv_all[:, c0 + C_NOPE:c0 + C_HEAD_PAD].T.astype(BF16)


def _c_prep(proj, cos_t, sin_t, cq_g, ckv_g, w_uq, w_ukv, q_g, k_g):
    bsz, seq, _ = proj.shape
    tp = KEY_TILE
    per_q = ATTN_COLS // tp
    nh = C_HEADS
    lead = C_Q_LORA + C_KV_LORA + LANES
    pad_rope = lambda g: jnp.concatenate([g, jnp.zeros((LANES - C_ROPE,), F32)]).reshape(1, LANES)
    wq = w_uq.reshape(C_Q_LORA, nh, C_QK)
    wq = jnp.concatenate([wq, jnp.zeros((C_Q_LORA, nh, C_HEAD_PAD - C_QK), w_uq.dtype)], axis=2)
    wq = wq.reshape(C_Q_LORA, nh * C_HEAD_PAD).astype(BF16)
    full = lambda shape: pl.BlockSpec(shape, lambda b, i: (0,) * len(shape))
    return pl.pallas_call(
        _c_prep_kernel,
        grid=(bsz, seq // tp),
        in_specs=[
            pl.BlockSpec((1, tp, lead), lambda b, i: (b, i, 0)),
            pl.BlockSpec((tp, LANES), lambda b, i: (i, 0)),
            pl.BlockSpec((tp, LANES), lambda b, i: (i, 0)),
            full((1, C_Q_LORA)), full((1, C_KV_LORA)),
            full((C_Q_LORA, nh * C_HEAD_PAD)), full((C_KV_LORA, nh * C_HEAD_PAD)),
            full((1, C_NOPE)), full((1, LANES)), full((1, C_NOPE)), full((1, LANES)),
        ],
        out_specs=[
            pl.BlockSpec((1, nh, 1, C_HEAD_PAD, tp), lambda b, i: (b, 0, i // per_q, 0, i % per_q)),
            pl.BlockSpec((1, tp, nh * C_HEAD_PAD), lambda b, i: (b, i, 0)),
            pl.BlockSpec((1, nh, 1, C_V, tp), lambda b, i: (b, 0, i, 0, 0)),
        ],
        out_shape=[
            jax.ShapeDtypeStruct((bsz, nh, seq // ATTN_COLS, C_HEAD_PAD, ATTN_COLS), BF16),
            jax.ShapeDtypeStruct((bsz, seq, nh * C_HEAD_PAD), BF16),
            jax.ShapeDtypeStruct((bsz, nh, seq // tp, C_V, tp), BF16),
        ],
        compiler_params=_cparams(("parallel", "parallel")),
        name="c_prep",
    )(proj, cos_t, sin_t, cq_g.reshape(1, -1), ckv_g.reshape(1, -1), wq, w_ukv.astype(BF16),
      q_g[:C_NOPE].reshape(1, -1), pad_rope(q_g[C_NOPE:]), k_g[:C_NOPE].reshape(1, -1), pad_rope(k_g[C_NOPE:]))


def _layer_a(x, tabs, norm_g, w_in, q_g, k_g, lq1, lk1, lq2, lk2, sub_g, w_out, lambda_init):
    bsz, seq, d = x.shape
    x2d = x.reshape(bsz * seq, d)
    n_in = w_in.shape[1]
    proj = _inproj(x2d, norm_g, w_in.astype(BF16), jnp.zeros((n_in,), F32), 1024).reshape(bsz, seq, n_in)
    qt, k, vt = _a_prep(proj, tabs[0], tabs[1], q_g, k_g)
    gate_blk0 = (n_in - A_HEADS * A_V_DIM) // LANES
    y = _attention(qt, k, vt, proj, gate_blk0, 2, (lq1, lk1, lq2, lk2), sub_g, lambda_init)
    return _outproj(y.reshape(bsz * seq, -1), w_out.astype(BF16), x2d).reshape(bsz, seq, d)


def _layer_b(x, norm_g, w_in, b_in, conv_w, conv_b, ln_g, ln_b, w_out):
    bsz, seq, d = x.shape
    x2d = x.reshape(bsz * seq, d)
    n_in = w_in.shape[1]
    proj = _inproj(x2d, norm_g, w_in.astype(BF16), b_in, 1024).reshape(bsz, seq, n_in)
    y = _conv_module(proj, conv_w, conv_b, ln_g, ln_b)
    return _outproj(y.reshape(bsz * seq, -1), w_out.astype(BF16), x2d).reshape(bsz, seq, d)


def _layer_c(x, tabs, norm_g, w_in, cq_g, w_uq, ckv_g, w_ukv, q_g, k_g, w_out):
    bsz, seq, d = x.shape
    x2d = x.reshape(bsz * seq, d)
    n_lead = C_Q_LORA + C_KV_LORA + C_ROPE
    w = jnp.concatenate([w_in[:, :n_lead], jnp.zeros((d, LANES - C_ROPE), w_in.dtype), w_in[:, n_lead:]], axis=1)
    n_in = w.shape[1]
    proj = _inproj(x2d, norm_g, w.astype(BF16), jnp.zeros((n_in,), F32), 512).reshape(bsz, seq, n_in)
    qt, k, vt = _c_prep(proj, tabs[0], tabs[1], cq_g, ckv_g, w_uq, w_ukv, q_g, k_g)
    gate_blk0 = (n_lead + LANES - C_ROPE) // LANES
    y = _attention(qt, k, vt, proj, gate_blk0, 1)
    return _outproj(y.reshape(bsz * seq, -1), w_out.astype(BF16), x2d).reshape(bsz, seq, d)


def kernel(x, a_norm_g, a_w_in, a_q_norm_g, a_k_norm_g, a_lam_q1, a_lam_k1, a_lam_q2, a_lam_k2, a_sub_norm_g, a_w_out, b_norm_g, b_w_in, b_b_in, b_conv_w, b_conv_b, b_ln_g, b_ln_b, b_w_out, c_norm_g, c_w_in, c_cq_norm_g, c_w_uq, c_ckv_norm_g, c_w_ukv, c_q_norm_g, c_k_norm_g, c_w_out):
    depth = a_norm_g.shape[0] + b_norm_g.shape[0] + c_norm_g.shape[0]
    tabs = _rope_tables(x.shape[1])
    for i in range(depth):
        kind, j = i % N_MIXERS, i // N_MIXERS
        if kind == 0:
            lambda_init = 0.8 - 0.6 * math.exp(-0.3 * i)
            x = _layer_a(x, tabs, a_norm_g[j], a_w_in[j], a_q_norm_g[j], a_k_norm_g[j], a_lam_q1[j],
                         a_lam_k1[j], a_lam_q2[j], a_lam_k2[j], a_sub_norm_g[j], a_w_out[j], lambda_init)
        elif kind == 1:
            x = _layer_b(x, b_norm_g[j], b_w_in[j], b_b_in[j], b_conv_w[j], b_conv_b[j], b_ln_g[j],
                         b_ln_b[j], b_w_out[j])
        else:
            x = _layer_c(x, tabs, c_norm_g[j], c_w_in[j], c_cq_norm_g[j], c_w_uq[j], c_ckv_norm_g[j],
                         c_w_ukv[j], c_q_norm_g[j], c_k_norm_g[j], c_w_out[j])
    return x
```

```python
import functools
import math

import numpy as np
import jax
import jax.numpy as jnp
from jax import lax
from jax.experimental import pallas as pl
from jax.experimental.pallas import tpu as pltpu

F32 = jnp.float32
BF16 = jnp.bfloat16

NORM_EPS = 1e-6
ROPE_THETA = 10000.0
CHUNK_SHIFT = 6
N_MIXERS = 3

A_HEADS = 8
A_QK_DIM = 64
A_V_DIM = 128
C_HEADS = 8
C_NOPE = 128
C_ROPE = 64
C_QK = C_NOPE + C_ROPE
C_V = 128
C_Q_LORA = 256
C_KV_LORA = 128
C_HEAD_PAD = 256
B_CONV = 31
CONV_HALO = 32

LANES = 128
SUBLANES = 8
KEY_TILE = 256
ATTN_COLS = 512
ROW_TILE = 1024
CONV_TILE = 256
CONV_ROWS = 128
VMEM_LIMIT = 48 * 1024 * 1024
NEG_BIG = -1e30
LOG2E = math.log2(math.e)


def _cparams(sem):
    return pltpu.CompilerParams(dimension_semantics=sem, vmem_limit_bytes=VMEM_LIMIT)


def _sigmoid(x):
    return 1.0 / (1.0 + jnp.exp(-x))


def _inproj_kernel(x_ref, g_ref, w_ref, b_ref, o_ref, h_scr):
    @pl.when(pl.program_id(1) == 0)
    def _():
        x = x_ref[...]
        ms = jnp.mean(x * x, axis=-1, keepdims=True)
        h_scr[...] = (x * lax.rsqrt(ms + NORM_EPS) * g_ref[...]).astype(BF16)

    acc = jnp.dot(h_scr[...], w_ref[...], preferred_element_type=F32)
    o_ref[...] = (acc + b_ref[...]).astype(o_ref.dtype)


def _inproj(x2d, g, w, bias, tn):
    n_tok, d = x2d.shape
    n = w.shape[1]
    tm = min(ROW_TILE, n_tok)
    return pl.pallas_call(
        _inproj_kernel,
        grid=(n_tok // tm, n // tn),
        in_specs=[
            pl.BlockSpec((tm, d), lambda i, j: (i, 0)),
            pl.BlockSpec((1, d), lambda i, j: (0, 0)),
            pl.BlockSpec((d, tn), lambda i, j: (0, j)),
            pl.BlockSpec((1, tn), lambda i, j: (0, j)),
        ],
        out_specs=pl.BlockSpec((tm, tn), lambda i, j: (i, j)),
        out_shape=jax.ShapeDtypeStruct((n_tok, n), BF16),
        scratch_shapes=[pltpu.VMEM((tm, d), BF16)],
        compiler_params=_cparams(("parallel", "arbitrary")),
        name="inproj",
    )(x2d, g.reshape(1, d), w, bias.reshape(1, n))


def _outproj_kernel(y_ref, w_ref, x_ref, o_ref):
    o_ref[...] = x_ref[...] + jnp.dot(y_ref[...], w_ref[...], preferred_element_type=F32)


def _outproj(y2d, w, x2d):
    n_tok, k = y2d.shape
    d = w.shape[1]
    tm = min(ROW_TILE, n_tok)
    return pl.pallas_call(
        _outproj_kernel,
        grid=(n_tok // tm,),
        in_specs=[
            pl.BlockSpec((tm, k), lambda i: (i, 0)),
            pl.BlockSpec((k, d), lambda i: (0, 0)),
            pl.BlockSpec((tm, d), lambda i: (i, 0)),
        ],
        out_specs=pl.BlockSpec((tm, d), lambda i: (i, 0)),
        out_shape=jax.ShapeDtypeStruct((n_tok, d), F32),
        compiler_params=_cparams(("parallel",)),
        name="outproj",
    )(y2d, w, x2d)


def _rope_tables(seq):
    half = A_QK_DIM // 2
    inv = 1.0 / (ROPE_THETA ** (jnp.arange(0, A_QK_DIM, 2, dtype=F32) / A_QK_DIM))
    ang = jnp.arange(seq, dtype=F32)[:, None] * inv[None, :]
    cos, sin = jnp.cos(ang), jnp.sin(ang)
    cos_t = jnp.tile(cos, (1, LANES // half))
    sin_t = jnp.tile(jnp.concatenate([-sin, sin], axis=1), (1, LANES // A_QK_DIM))
    return cos_t, sin_t


def _rope64(x, cos_t, sin_t):
    lane = lax.broadcasted_iota(jnp.int32, x.shape, 1)
    first_half = (lane & (A_QK_DIM - 1)) < (A_QK_DIM // 2)
    partner = jnp.where(first_half,
                        pltpu.roll(x, LANES - A_QK_DIM // 2, 1),
                        pltpu.roll(x, A_QK_DIM // 2, 1))
    return x * cos_t + partner * sin_t


def _group_mean_matrix():
    lane = np.arange(LANES)
    return ((lane[:, None] // A_QK_DIM) == (lane[None, :] // A_QK_DIM)).astype(np.float32) / A_QK_DIM


def _rope_partner_matrix():
    half = A_QK_DIM // 2
    dst = np.arange(LANES)
    src = np.where((dst % A_QK_DIM) < half, dst + half, dst - half)
    p = np.zeros((LANES, LANES), np.float32)
    p[src, dst] = 1.0
    return p


def _a_prep_kernel(q_ref, k_ref, v_ref, cos_ref, sin_ref, cost_ref, sint_ref, qg_ref, kg_ref, gmat_ref, pmat_ref,
                   qt_ref, ko_ref, vt_ref):
    cos_n, sin_n = cos_ref[...], sin_ref[...]
    cos_t, sin_t = cost_ref[...], sint_ref[...]
    half, grp = A_QK_DIM // 2, A_QK_DIM
    for h in range(A_HEADS):
        cols = slice(h * 2 * grp, (h + 1) * 2 * grp)
        q_t = q_ref[0, :, cols].astype(F32).T
        sq = q_t * q_t
        parts = []
        for g0 in (0, grp):
            ms = jnp.sum(sq[g0:g0 + grp], axis=0, keepdims=True) * (1.0 / grp)
            parts.append(q_t[g0:g0 + grp] * lax.rsqrt(ms + NORM_EPS))
        qn = jnp.concatenate(parts, axis=0) * qg_ref[...]
        partner = jnp.concatenate([qn[half:grp], qn[0:half], qn[grp + half:], qn[grp:grp + half]], axis=0)
        qr = (qn * cos_t + partner * sin_t).astype(BF16)
        row = lax.broadcasted_iota(jnp.int32, qr.shape, 0)
        zero = jnp.zeros_like(qr)
        qt_ref[0, h, 0] = jnp.concatenate([jnp.where(row < grp, qr, zero), jnp.where(row >= grp, qr, zero)], axis=1)
        kx = k_ref[0, :, cols].astype(F32)
        ms = jnp.dot((kx * kx).astype(BF16), gmat_ref[...], preferred_element_type=F32)
        kn = kx * lax.rsqrt(ms + NORM_EPS) * kg_ref[...]
        hi = kn.astype(BF16)
        lo = (kn - hi.astype(F32)).astype(BF16)
        partner = (jnp.dot(hi, pmat_ref[...], preferred_element_type=F32)
                   + jnp.dot(lo, pmat_ref[...], preferred_element_type=F32))
        ko_ref[0, :, cols] = (kn * cos_n + partner * sin_n).astype(BF16)
        vt_ref[0, h, 0] = v_ref[0, :, cols].astype(F32).T.astype(BF16)


def _a_prep(proj, cos_n, sin_n, q_g, k_g):
    bsz, seq, _ = proj.shape
    tp = KEY_TILE
    nh = A_HEADS
    hd = 2 * A_QK_DIM
    w = nh * hd
    q_gain = jnp.broadcast_to((jnp.tile(q_g, 2) * (A_QK_DIM ** -0.5 * LOG2E))[:, None], (hd, tp))
    tok_spec = lambda c: pl.BlockSpec((1, tp, w), lambda b, i: (b, i, c))
    const = lambda shape: pl.BlockSpec(shape, lambda b, i: (0,) * len(shape))
    return pl.pallas_call(
        _a_prep_kernel,
        grid=(bsz, seq // tp),
        in_specs=[tok_spec(0), tok_spec(1), tok_spec(2),
                  pl.BlockSpec((tp, LANES), lambda b, i: (i, 0)), pl.BlockSpec((tp, LANES), lambda b, i: (i, 0)),
                  pl.BlockSpec((LANES, tp), lambda b, i: (0, i)), pl.BlockSpec((LANES, tp), lambda b, i: (0, i)),
                  const((hd, tp)), const((1, hd)), const((LANES, LANES)), const((LANES, LANES))],
        out_specs=[
            pl.BlockSpec((1, nh, 1, hd, 2 * tp), lambda b, i: (b, 0, i, 0, 0)),
            pl.BlockSpec((1, tp, w), lambda b, i: (b, i, 0)),
            pl.BlockSpec((1, nh, 1, A_V_DIM, tp), lambda b, i: (b, 0, i, 0, 0)),
        ],
        out_shape=[
            jax.ShapeDtypeStruct((bsz, nh, seq // tp, hd, 2 * tp), BF16),
            jax.ShapeDtypeStruct((bsz, seq, w), BF16),
            jax.ShapeDtypeStruct((bsz, nh, seq // tp, A_V_DIM, tp), BF16),
        ],
        compiler_params=_cparams(("parallel", "parallel")),
        name="a_prep",
    )(proj, proj, proj, cos_n, sin_n, cos_n.T, sin_n.T, q_gain, jnp.tile(k_g, 2).reshape(1, hd),
      jnp.asarray(_group_mean_matrix(), BF16), jnp.asarray(_rope_partner_matrix(), BF16))


def _pair_schedule(n_qtiles, tq_tok):
    per_q = tq_tok // KEY_TILE
    n_slots = (n_qtiles + 1) * per_q
    sel, ki, bi = [], [], []
    for g in range(n_qtiles // 2):
        lists = []
        for which, j in enumerate((g, n_qtiles - 1 - g)):
            lists.append([(which, i, max(i - j * per_q + 1, 0)) for i in range((j + 1) * per_q)])
        short, long_ = lists
        merged = []
        for n, item in enumerate(long_):
            merged.append(item)
            if n < len(short):
                merged.append(short[n])
        assert len(merged) == n_slots
        for w, i, f in merged:
            sel.append(w), ki.append(i), bi.append(f)
    as_i32 = lambda v: np.asarray(v, np.int32)
    return as_i32(sel), as_i32(ki), as_i32(bi), n_slots


def _mask_bias(tq_tok, n_maps):
    per_q = tq_tok // KEY_TILE
    kk = np.arange(KEY_TILE)[:, None]
    q_tok = (np.arange(tq_tok * n_maps) % tq_tok)[None, :]
    bias = np.zeros((per_q + 1, KEY_TILE, tq_tok * n_maps), np.float32)
    for f in range(per_q):
        masked = ((kk + f * KEY_TILE) >> CHUNK_SHIFT) > (q_tok >> CHUNK_SHIFT)
        bias[f + 1] = np.where(masked, NEG_BIG, 0.0)
    return bias


def _attn_kernel(sel_ref, ki_ref, bi_ref, qt_ref, k_ref, vt_ref, gate_ref, bias_ref, *rest,
                 n_maps, lambda_init, n_slots, n_qtiles):
    if n_maps == 2:
        lq1_ref, lk1_ref, lq2_ref, lk2_ref, subg_ref, o_ref, s_a, s_b, p_a, p_b, acc_scr = rest
    else:
        o_ref, s_a, s_b, p_a, p_b, acc_scr = rest
    s_buf, p_buf = (s_a, s_b), (p_a, p_b)
    tq_tok = ATTN_COLS // n_maps
    g = pl.program_id(2)
    base = g * n_slots
    tiles = (g, n_qtiles - 1 - g)

    acc_scr[...] = jnp.zeros(acc_scr.shape, F32)
    m = [jnp.full((1, ATTN_COLS), NEG_BIG, F32)] * 2
    l = [jnp.zeros((1, ATTN_COLS), F32)] * 2
    col_max, alphas = {}, {}

    def scores(n):
        w, i = sel_ref[base + n], ki_ref[base + n]
        j = jnp.where(w == 0, tiles[0], tiles[1])
        kb = k_ref[0, pl.ds(pl.multiple_of(i * KEY_TILE, KEY_TILE), KEY_TILE), :]
        sv = jnp.dot(kb, qt_ref[0, 0, j], preferred_element_type=F32) + bias_ref[bi_ref[base + n]]
        s_buf[n % 2][...] = sv
        col_max[n] = jnp.max(sv, axis=0, keepdims=True)

    def softmax(n):
        first = sel_ref[base + n] == 0
        m_old = jnp.where(first, m[0], m[1])
        m_new = jnp.maximum(m_old, col_max.pop(n))
        alpha = jnp.exp2(m_old - m_new)
        p = jnp.exp2(s_buf[n % 2][...] - m_new)
        l_new = alpha * jnp.where(first, l[0], l[1]) + jnp.sum(p, axis=0, keepdims=True)
        p_buf[n % 2][...] = p.astype(BF16)
        alphas[n] = alpha
        m[0], m[1] = jnp.where(first, m_new, m[0]), jnp.where(first, m[1], m_new)
        l[0], l[1] = jnp.where(first, l_new, l[0]), jnp.where(first, l[1], l_new)

    def accumulate(n):
        w, i = sel_ref[base + n], ki_ref[base + n]
        pv = jnp.dot(vt_ref[0, 0, i], p_buf[n % 2][...], preferred_element_type=F32)
        acc_scr[w] = alphas.pop(n) * acc_scr[w] + pv

    for t in range(n_slots + 2):
        if t >= 2:
            accumulate(t - 2)
        if t < n_slots:
            scores(t)
        if 1 <= t <= n_slots:
            softmax(t - 1)

    for w in range(2):
        o_cat = acc_scr[w] / l[w]
        if n_maps == 2:
            f = lambda r: jnp.sum(r, axis=-1, keepdims=True)
            lam = (jnp.exp(f(lq1_ref[...] * lk1_ref[...])) - jnp.exp(f(lq2_ref[...] * lk2_ref[...]))
                   + lambda_init)
            o = (o_cat[:, :tq_tok] - lam * o_cat[:, tq_tok:]).T
            ms = jnp.mean(o * o, axis=-1, keepdims=True)
            o = o * lax.rsqrt(ms + NORM_EPS) * subg_ref[...] * (1.0 - lambda_init)
        else:
            o = o_cat.T
        rows = pl.ds(pl.multiple_of(tiles[w] * tq_tok, tq_tok), tq_tok)
        gt = gate_ref[0, rows, :].astype(F32)
        o_ref[0, rows, :] = (o * (gt * _sigmoid(gt))).astype(o_ref.dtype)


def _attention(qt, k, vt, gate_src, gate_blk0, n_maps, lam_params=(), sub_g=None, lambda_init=0.0):
    bsz, nh, nq, dk, cols = qt.shape
    _, _, nkv, dv, tk = vt.shape
    assert cols == ATTN_COLS and tk == KEY_TILE and nq % 2 == 0
    seq = nkv * tk
    tq_tok = cols // n_maps
    sel, ki, bi, n_slots = _pair_schedule(nq, tq_tok)
    bias = _mask_bias(tq_tok, n_maps)
    in_specs = [
        pl.BlockSpec((1, 1, nq, dk, cols), lambda b, h, g, *_: (b, h, 0, 0, 0)),
        pl.BlockSpec((1, seq, dk), lambda b, h, g, *_: (b, 0, h)),
        pl.BlockSpec((1, 1, nkv, dv, tk), lambda b, h, g, *_: (b, h, 0, 0, 0)),
        pl.BlockSpec((1, seq, dv), lambda b, h, g, *_: (b, 0, gate_blk0 + h)),
        pl.BlockSpec(bias.shape, lambda b, h, g, *_: (0, 0, 0)),
    ]
    args = [qt, k, vt, gate_src, jnp.asarray(bias)]
    if n_maps == 2:
        for p in lam_params:
            in_specs.append(pl.BlockSpec((1, A_QK_DIM), lambda b, h, g, *_: (0, 0)))
            args.append(p.reshape(1, A_QK_DIM))
        in_specs.append(pl.BlockSpec((1, dv), lambda b, h, g, *_: (0, 0)))
        args.append(sub_g.reshape(1, dv))
    return pl.pallas_call(
        functools.partial(_attn_kernel, n_maps=n_maps, lambda_init=lambda_init, n_slots=n_slots, n_qtiles=nq),
        grid_spec=pltpu.PrefetchScalarGridSpec(
            num_scalar_prefetch=3,
            grid=(bsz, nh, nq // 2),
            in_specs=in_specs,
            out_specs=pl.BlockSpec((1, seq, dv), lambda b, h, g, *_: (b, 0, h)),
            scratch_shapes=[
                pltpu.VMEM((tk, cols), F32), pltpu.VMEM((tk, cols), F32),
                pltpu.VMEM((tk, cols), BF16), pltpu.VMEM((tk, cols), BF16),
                pltpu.VMEM((2, dv, cols), F32),
            ],
        ),
        out_shape=jax.ShapeDtypeStruct((bsz, seq, nh * dv), BF16),
        compiler_params=_cparams(("parallel", "parallel", "arbitrary")),
        name="attn%d" % n_maps,
    )(jnp.asarray(sel), jnp.asarray(ki), jnp.asarray(bi), *args)


def _conv_kernel(a_ref, b_ref, gate_ref, w_ref, cb_ref, lng_ref, lnb_ref, o_ref, u_scr, y_scr, sh_scr):
    tc = a_ref.shape[1]
    width = u_scr.shape[1]
    j = pl.program_id(1)

    @pl.when(j == 0)
    def _():
        u_scr[0:CONV_HALO, :] = jnp.zeros((CONV_HALO, width), F32)

    @pl.when(j > 0)
    def _():
        u_scr[0:CONV_HALO, :] = u_scr[tc:tc + CONV_HALO, :]

    u_scr[CONV_HALO:CONV_HALO + tc, :] = a_ref[0].astype(F32) * _sigmoid(b_ref[0].astype(F32))
    first = CONV_HALO - (B_CONV - 1)
    for c0 in range(0, width, LANES):
        cols = slice(c0, c0 + LANES)
        for r0 in range(0, tc, CONV_ROWS):
            acc = jnp.broadcast_to(cb_ref[:, cols], (CONV_ROWS, LANES))
            for res in range(SUBLANES):
                taps = [t for t in range(B_CONV) if (first + t) % SUBLANES == res]
                span = (first + taps[-1]) // SUBLANES * SUBLANES + CONV_ROWS
                sh_scr[0:span, :] = u_scr[r0 + res:r0 + res + span, cols]
                for t in taps:
                    off = (first + t) // SUBLANES * SUBLANES
                    acc = acc + sh_scr[off:off + CONV_ROWS, :] * w_ref[t:t + 1, cols]
            y_scr[r0:r0 + CONV_ROWS, cols] = acc
    acc = y_scr[...]
    mu = jnp.mean(acc, axis=-1, keepdims=True)
    xc = acc - mu
    var = jnp.mean(xc * xc, axis=-1, keepdims=True)
    y = xc * lax.rsqrt(var + NORM_EPS) * lng_ref[...] + lnb_ref[...]
    y = y * _sigmoid(y)
    g = gate_ref[0].astype(F32)
    o_ref[0] = (y * (g * _sigmoid(g))).astype(o_ref.dtype)


def _conv_module(proj, conv_w, conv_b, ln_g, ln_b):
    bsz, seq, w3 = proj.shape
    w = w3 // 3
    tc = min(CONV_TILE, seq)
    blk = lambda c: pl.BlockSpec((1, tc, w), lambda b, j: (b, j, c))
    par = pl.BlockSpec((1, w), lambda b, j: (0, 0))
    return pl.pallas_call(
        _conv_kernel,
        grid=(bsz, seq // tc),
        in_specs=[blk(0), blk(1), blk(2), pl.BlockSpec((B_CONV, w), lambda b, j: (0, 0)), par, par, par],
        out_specs=pl.BlockSpec((1, tc, w), lambda b, j: (b, j, 0)),
        out_shape=jax.ShapeDtypeStruct((bsz, seq, w), BF16),
        scratch_shapes=[pltpu.VMEM((tc + CONV_HALO, w), F32), pltpu.VMEM((tc, w), F32),
                        pltpu.VMEM((CONV_ROWS + CONV_HALO, LANES), F32)],
        compiler_params=_cparams(("parallel", "arbitrary")),
        name="conv",
    )(proj, proj, proj, conv_w, conv_b.reshape(1, w), ln_g.reshape(1, w), ln_b.reshape(1, w))


def _c_prep_kernel(p_ref, cos_ref, sin_ref, cqg_ref, ckvg_ref, wuq_ref, wukv_ref,
                   qgn_ref, qgr_ref, kgn_ref, kgr_ref, qt_ref, ko_ref, vt_ref):
    cos_t, sin_t = cos_ref[...], sin_ref[...]
    p = p_ref[0].astype(F32)

    def rms(x, g, width):
        ms = jnp.sum(x * x, axis=-1, keepdims=True) * (1.0 / width)
        return x * lax.rsqrt(ms + NORM_EPS) * g

    c_q = rms(p[:, :C_Q_LORA], cqg_ref[...], C_Q_LORA).astype(BF16)
    c_kv = rms(p[:, C_Q_LORA:C_Q_LORA + C_KV_LORA], ckvg_ref[...], C_KV_LORA).astype(BF16)
    q_all = jnp.dot(c_q, wuq_ref[...], preferred_element_type=F32)
    kv_all = jnp.dot(c_kv, wukv_ref[...], preferred_element_type=F32)
    k_rope = _rope64(rms(p[:, C_Q_LORA + C_KV_LORA:], kgr_ref[...], C_ROPE), cos_t, sin_t).astype(BF16)
    scale = C_QK ** -0.5 * LOG2E
    for h in range(C_HEADS):
        c0 = h * C_HEAD_PAD
        qn = rms(q_all[:, c0:c0 + C_NOPE], qgn_ref[...], C_NOPE) * scale
        qr = _rope64(rms(q_all[:, c0 + C_NOPE:c0 + C_HEAD_PAD], qgr_ref[...], C_ROPE), cos_t, sin_t) * scale
        qt_ref[0, h, 0, 0:C_NOPE, :] = qn.T.astype(BF16)
        qt_ref[0, h, 0, C_NOPE:C_HEAD_PAD, :] = qr.T.astype(BF16)
        ko_ref[0, :, c0:c0 + C_NOPE] = rms(kv_all[:, c0:c0 + C_NOPE], kgn_ref[...], C_NOPE).astype(BF16)
        ko_ref[0, :, c0 + C_NOPE:c0 + C_HEAD_PAD] = k_rope
        vt_ref[0, h, 0] = kv_all[:, c0 + C_NOPE:c0 + C_HEAD_PAD].T.astype(BF16)


def _c_prep(proj, cos_t, sin_t, cq_g, ckv_g, w_uq, w_ukv, q_g, k_g):
    bsz, seq, _ = proj.shape
    tp = KEY_TILE
    per_q = ATTN_COLS // tp
    nh = C_HEADS
    lead = C_Q_LORA + C_KV_LORA + LANES
    pad_rope = lambda g: jnp.concatenate([g, jnp.zeros((LANES - C_ROPE,), F32)]).reshape(1, LANES)
    wq = w_uq.reshape(C_Q_LORA, nh, C_QK)
    wq = jnp.concatenate([wq, jnp.zeros((C_Q_LORA, nh, C_HEAD_PAD - C_QK), w_uq.dtype)], axis=2)
    wq = wq.reshape(C_Q_LORA, nh * C_HEAD_PAD).astype(BF16)
    full = lambda shape: pl.BlockSpec(shape, lambda b, i: (0,) * len(shape))
    return pl.pallas_call(
        _c_prep_kernel,
        grid=(bsz, seq // tp),
        in_specs=[
            pl.BlockSpec((1, tp, lead), lambda b, i: (b, i, 0)),
            pl.BlockSpec((tp, LANES), lambda b, i: (i, 0)),
            pl.BlockSpec((tp, LANES), lambda b, i: (i, 0)),
            full((1, C_Q_LORA)), full((1, C_KV_LORA)),
            full((C_Q_LORA, nh * C_HEAD_PAD)), full((C_KV_LORA, nh * C_HEAD_PAD)),
            full((1, C_NOPE)), full((1, LANES)), full((1, C_NOPE)), full((1, LANES)),
        ],
        out_specs=[
            pl.BlockSpec((1, nh, 1, C_HEAD_PAD, tp), lambda b, i: (b, 0, i // per_q, 0, i % per_q)),
            pl.BlockSpec((1, tp, nh * C_HEAD_PAD), lambda b, i: (b, i, 0)),
            pl.BlockSpec((1, nh, 1, C_V, tp), lambda b, i: (b, 0, i, 0, 0)),
        ],
        out_shape=[
            jax.ShapeDtypeStruct((bsz, nh, seq // ATTN_COLS, C_HEAD_PAD, ATTN_COLS), BF16),
            jax.ShapeDtypeStruct((bsz, seq, nh * C_HEAD_PAD), BF16),
            jax.ShapeDtypeStruct((bsz, nh, seq // tp, C_V, tp), BF16),
        ],
        compiler_params=_cparams(("parallel", "parallel")),
        name="c_prep",
    )(proj, cos_t, sin_t, cq_g.reshape(1, -1), ckv_g.reshape(1, -1), wq, w_ukv.astype(BF16),
      q_g[:C_NOPE].reshape(1, -1), pad_rope(q_g[C_NOPE:]), k_g[:C_NOPE].reshape(1, -1), pad_rope(k_g[C_NOPE:]))


def _layer_a(x, tabs, norm_g, w_in, q_g, k_g, lq1, lk1, lq2, lk2, sub_g, w_out, lambda_init):
    bsz, seq, d = x.shape
    x2d = x.reshape(bsz * seq, d)
    n_in = w_in.shape[1]
    proj = _inproj(x2d, norm_g, w_in.astype(BF16), jnp.zeros((n_in,), F32), 1024).reshape(bsz, seq, n_in)
    qt, k, vt = _a_prep(proj, tabs[0], tabs[1], q_g, k_g)
    gate_blk0 = (n_in - A_HEADS * A_V_DIM) // LANES
    y = _attention(qt, k, vt, proj, gate_blk0, 2, (lq1, lk1, lq2, lk2), sub_g, lambda_init)
    return _outproj(y.reshape(bsz * seq, -1), w_out.astype(BF16), x2d).reshape(bsz, seq, d)


def _layer_b(x, norm_g, w_in, b_in, conv_w, conv_b, ln_g, ln_b, w_out):
    bsz, seq, d = x.shape
    x2d = x.reshape(bsz * seq, d)
    n_in = w_in.shape[1]
    proj = _inproj(x2d, norm_g, w_in.astype(BF16), b_in, 1024).reshape(bsz, seq, n_in)
    y = _conv_module(proj, conv_w, conv_b, ln_g, ln_b)
    return _outproj(y.reshape(bsz * seq, -1), w_out.astype(BF16), x2d).reshape(bsz, seq, d)


def _layer_c(x, tabs, norm_g, w_in, cq_g, w_uq, ckv_g, w_ukv, q_g, k_g, w_out):
    bsz, seq, d = x.shape
    x2d = x.reshape(bsz * seq, d)
    n_lead = C_Q_LORA + C_KV_LORA + C_ROPE
    w = jnp.concatenate([w_in[:, :n_lead], jnp.zeros((d, LANES - C_ROPE), w_in.dtype), w_in[:, n_lead:]], axis=1)
    n_in = w.shape[1]
    proj = _inproj(x2d, norm_g, w.astype(BF16), jnp.zeros((n_in,), F32), 512).reshape(bsz, seq, n_in)
    qt, k, vt = _c_prep(proj, tabs[0], tabs[1], cq_g, ckv_g, w_uq, w_ukv, q_g, k_g)
    gate_blk0 = (n_lead + LANES - C_ROPE) // LANES
    y = _attention(qt, k, vt, proj, gate_blk0, 1)
    return _outproj(y.reshape(bsz * seq, -1), w_out.astype(BF16), x2d).reshape(bsz, seq, d)


def kernel(x, a_norm_g, a_w_in, a_q_norm_g, a_k_norm_g, a_lam_q1, a_lam_k1, a_lam_q2, a_lam_k2, a_sub_norm_g, a_w_out, b_norm_g, b_w_in, b_b_in, b_conv_w, b_conv_b, b_ln_g, b_ln_b, b_w_out, c_norm_g, c_w_in, c_cq_norm_g, c_w_uq, c_ckv_norm_g, c_w_ukv, c_q_norm_g, c_k_norm_g, c_w_out):
    depth = a_norm_g.shape[0] + b_norm_g.shape[0] + c_norm_g.shape[0]
    tabs = _rope_tables(x.shape[1])
    for i in range(depth):
        kind, j = i % N_MIXERS, i // N_MIXERS
        if kind == 0:
            lambda_init = 0.8 - 0.6 * math.exp(-0.3 * i)
            x = _layer_a(x, tabs, a_norm_g[j], a_w_in[j], a_q_norm_g[j], a_k_norm_g[j], a_lam_q1[j],
                         a_lam_k1[j], a_lam_q2[j], a_lam_k2[j], a_sub_norm_g[j], a_w_out[j], lambda_init)
        elif kind == 1:
            x = _layer_b(x, b_norm_g[j], b_w_in[j], b_b_in[j], b_conv_w[j], b_conv_b[j], b_ln_g[j],
                         b_ln_b[j], b_w_out[j])
        else:
            x = _layer_c(x, tabs, c_norm_g[j], c_w_in[j], c_cq_norm_g[j], c_w_uq[j], c_ckv_norm_g[j],
                         c_w_ukv[j], c_q_norm_g[j], c_k_norm_g[j], c_w_out[j])
    return x
```

```python
import functools
import math

import numpy as np
import jax
import jax.numpy as jnp
from jax import lax
from jax.experimental import pallas as pl
from jax.experimental.pallas import tpu as pltpu

F32 = jnp.float32
BF16 = jnp.bfloat16

NORM_EPS = 1e-6
ROPE_THETA = 10000.0
CHUNK_SHIFT = 6
N_MIXERS = 3

A_HEADS = 8
A_QK_DIM = 64
A_V_DIM = 128
C_HEADS = 8
C_NOPE = 128
C_ROPE = 64
C_QK = C_NOPE + C_ROPE
C_V = 128
C_Q_LORA = 256
C_KV_LORA = 128
C_HEAD_PAD = 256
B_CONV = 31
CONV_HALO = 32

LANES = 128
SUBLANES = 8
PREP_TILE = 256
KEY_TILE = 512
ATTN_COLS = 512
ROW_TILE = 1024
CONV_TILE = 256
CONV_ROWS = 128
VMEM_LIMIT = 48 * 1024 * 1024
NEG_BIG = -1e30
LOG2E = math.log2(math.e)


def _cparams(sem):
    return pltpu.CompilerParams(dimension_semantics=sem, vmem_limit_bytes=VMEM_LIMIT)


def _sigmoid(x):
    return 1.0 / (1.0 + jnp.exp(-x))


def _inproj_kernel(x_ref, g_ref, w_ref, *rest):
    b_ref = rest[0] if len(rest) == 3 else None
    o_ref, h_scr = rest[-2:]

    @pl.when(pl.program_id(1) == 0)
    def _():
        x = x_ref[...]
        ms = jnp.mean(x * x, axis=-1, keepdims=True)
        h_scr[...] = (x * lax.rsqrt(ms + NORM_EPS) * g_ref[...]).astype(BF16)

    acc = jnp.dot(h_scr[...], w_ref[...], preferred_element_type=F32)
    if b_ref is not None:
        acc = acc + b_ref[...]
    o_ref[...] = acc.astype(o_ref.dtype)


def _inproj(x2d, g, w, bias, tn):
    n_tok, d = x2d.shape
    n = w.shape[1]
    tm = min(ROW_TILE, n_tok)
    in_specs = [
        pl.BlockSpec((tm, d), lambda i, j: (i, 0)),
        pl.BlockSpec((1, d), lambda i, j: (0, 0)),
        pl.BlockSpec((d, tn), lambda i, j: (0, j)),
    ]
    args = [x2d, g.reshape(1, d), w]
    if bias is not None:
        in_specs.append(pl.BlockSpec((1, tn), lambda i, j: (0, j)))
        args.append(bias.reshape(1, n))
    return pl.pallas_call(
        _inproj_kernel,
        grid=(n_tok // tm, n // tn),
        in_specs=in_specs,
        out_specs=pl.BlockSpec((tm, tn), lambda i, j: (i, j)),
        out_shape=jax.ShapeDtypeStruct((n_tok, n), BF16),
        scratch_shapes=[pltpu.VMEM((tm, d), BF16)],
        compiler_params=_cparams(("parallel", "arbitrary")),
        name="inproj",
    )(*args)


def _outproj_kernel(y_ref, w_ref, x_ref, o_ref):
    o_ref[...] = x_ref[...] + jnp.dot(y_ref[...], w_ref[...], preferred_element_type=F32)


def _outproj(y2d, w, x2d):
    n_tok, k = y2d.shape
    d = w.shape[1]
    tm = min(ROW_TILE, n_tok)
    return pl.pallas_call(
        _outproj_kernel,
        grid=(n_tok // tm,),
        in_specs=[
            pl.BlockSpec((tm, k), lambda i: (i, 0)),
            pl.BlockSpec((k, d), lambda i: (0, 0)),
            pl.BlockSpec((tm, d), lambda i: (i, 0)),
        ],
        out_specs=pl.BlockSpec((tm, d), lambda i: (i, 0)),
        out_shape=jax.ShapeDtypeStruct((n_tok, d), F32),
        compiler_params=_cparams(("parallel",)),
        name="outproj",
    )(y2d, w, x2d)


def _rope_tables(seq):
    half = A_QK_DIM // 2
    inv = 1.0 / (ROPE_THETA ** (jnp.arange(0, A_QK_DIM, 2, dtype=F32) / A_QK_DIM))
    ang = jnp.arange(seq, dtype=F32)[:, None] * inv[None, :]
    cos, sin = jnp.cos(ang), jnp.sin(ang)
    cos_t = jnp.tile(cos, (1, LANES // half))
    sin_t = jnp.tile(jnp.concatenate([-sin, sin], axis=1), (1, LANES // A_QK_DIM))
    return cos_t, sin_t


def _rope64(x, cos_t, sin_t):
    lane = lax.broadcasted_iota(jnp.int32, x.shape, 1)
    first_half = (lane & (A_QK_DIM - 1)) < (A_QK_DIM // 2)
    partner = jnp.where(first_half,
                        pltpu.roll(x, LANES - A_QK_DIM // 2, 1),
                        pltpu.roll(x, A_QK_DIM // 2, 1))
    return x * cos_t + partner * sin_t


def _group_mean_matrix():
    lane = np.arange(LANES)
    return ((lane[:, None] // A_QK_DIM) == (lane[None, :] // A_QK_DIM)).astype(np.float32) / A_QK_DIM


def _rope_partner_matrix():
    half = A_QK_DIM // 2
    dst = np.arange(LANES)
    src = np.where((dst % A_QK_DIM) < half, dst + half, dst - half)
    p = np.zeros((LANES, LANES), np.float32)
    p[src, dst] = 1.0
    return p


def _a_prep_kernel(q_ref, k_ref, v_ref, cos_ref, sin_ref, cost_ref, sint_ref, qg_ref, kg_ref, gmat_ref, pmat_ref,
                   qt_ref, ko_ref, vt_ref):
    cos_n, sin_n = cos_ref[...], sin_ref[...]
    cos_t, sin_t = cost_ref[...], sint_ref[...]
    half, grp = A_QK_DIM // 2, A_QK_DIM
    for h in range(A_HEADS):
        cols = slice(h * 2 * grp, (h + 1) * 2 * grp)
        q_t = q_ref[0, :, cols].astype(F32).T
        sq = q_t * q_t
        parts = []
        for g0 in (0, grp):
            ms = jnp.sum(sq[g0:g0 + grp], axis=0, keepdims=True) * (1.0 / grp)
            parts.append(q_t[g0:g0 + grp] * lax.rsqrt(ms + NORM_EPS))
        qn = jnp.concatenate(parts, axis=0) * qg_ref[...]
        partner = jnp.concatenate([qn[half:grp], qn[0:half], qn[grp + half:], qn[grp:grp + half]], axis=0)
        qr = (qn * cos_t + partner * sin_t).astype(BF16)
        row = lax.broadcasted_iota(jnp.int32, qr.shape, 0)
        zero = jnp.zeros_like(qr)
        qt_ref[0, h, 0] = jnp.concatenate([jnp.where(row < grp, qr, zero), jnp.where(row >= grp, qr, zero)], axis=1)
        kx = k_ref[0, :, cols].astype(F32)
        ms = jnp.dot((kx * kx).astype(BF16), gmat_ref[...], preferred_element_type=F32)
        kn = kx * lax.rsqrt(ms + NORM_EPS) * kg_ref[...]
        hi = kn.astype(BF16)
        lo = (kn - hi.astype(F32)).astype(BF16)
        partner = (jnp.dot(hi, pmat_ref[...], preferred_element_type=F32)
                   + jnp.dot(lo, pmat_ref[...], preferred_element_type=F32))
        ko_ref[0, :, cols] = (kn * cos_n + partner * sin_n).astype(BF16)
        vt_ref[0, h, 0] = v_ref[0, :, cols].astype(F32).T.astype(BF16)


def _a_prep(proj, cos_n, sin_n, q_g, k_g):
    bsz, seq, _ = proj.shape
    tp = PREP_TILE
    kt = KEY_TILE // tp
    nh = A_HEADS
    hd = 2 * A_QK_DIM
    w = nh * hd
    q_gain = jnp.broadcast_to((jnp.tile(q_g, 2) * (A_QK_DIM ** -0.5 * LOG2E))[:, None], (hd, tp))
    tok_spec = lambda c: pl.BlockSpec((1, tp, w), lambda b, i: (b, i, c))
    const = lambda shape: pl.BlockSpec(shape, lambda b, i: (0,) * len(shape))
    return pl.pallas_call(
        _a_prep_kernel,
        grid=(bsz, seq // tp),
        in_specs=[tok_spec(0), tok_spec(1), tok_spec(2),
                  pl.BlockSpec((tp, LANES), lambda b, i: (i, 0)), pl.BlockSpec((tp, LANES), lambda b, i: (i, 0)),
                  pl.BlockSpec((LANES, tp), lambda b, i: (0, i)), pl.BlockSpec((LANES, tp), lambda b, i: (0, i)),
                  const((hd, tp)), const((1, hd)), const((LANES, LANES)), const((LANES, LANES))],
        out_specs=[
            pl.BlockSpec((1, nh, 1, hd, 2 * tp), lambda b, i: (b, 0, i, 0, 0)),
            pl.BlockSpec((1, tp, w), lambda b, i: (b, i, 0)),
            pl.BlockSpec((1, nh, 1, A_V_DIM, tp), lambda b, i: (b, 0, i // kt, 0, i % kt)),
        ],
        out_shape=[
            jax.ShapeDtypeStruct((bsz, nh, seq // tp, hd, 2 * tp), BF16),
            jax.ShapeDtypeStruct((bsz, seq, w), BF16),
            jax.ShapeDtypeStruct((bsz, nh, seq // KEY_TILE, A_V_DIM, KEY_TILE), BF16),
        ],
        compiler_params=_cparams(("parallel", "parallel")),
        name="a_prep",
    )(proj, proj, proj, cos_n, sin_n, cos_n.T, sin_n.T, q_gain, jnp.tile(k_g, 2).reshape(1, hd),
      jnp.asarray(_group_mean_matrix(), BF16), jnp.asarray(_rope_partner_matrix(), BF16))


def _pair_schedule(n_qtiles, tq_tok):
    reps = KEY_TILE // tq_tok
    n_slots = 2 + (n_qtiles - 1) // reps
    sel, ki, bi = [], [], []
    for g in range(n_qtiles // 2):
        j_pair = (g, n_qtiles - 1 - g)
        merged = [(w, j_pair[w] // reps, j_pair[w] % reps) for w in range(2)]
        short, long_ = ([(w, i, 0) for i in range(j_pair[w] // reps)] for w in range(2))
        for n, item in enumerate(long_):
            merged.append(item)
            if n < len(short):
                merged.append(short[n])
        assert len(merged) == n_slots
        for w, i, f in merged:
            sel.append(w), ki.append(i), bi.append(f)
    as_i32 = lambda v: np.asarray(v, np.int32)
    return as_i32(sel), as_i32(ki), as_i32(bi), n_slots


def _mask_bias(tq_tok, n_maps):
    reps = KEY_TILE // tq_tok
    kk = np.arange(KEY_TILE)[:, None]
    q_tok = (np.arange(tq_tok * n_maps) % tq_tok)[None, :]
    bias = np.zeros((reps, KEY_TILE, tq_tok * n_maps), np.float32)
    for f in range(reps):
        masked = ((kk - f * tq_tok) >> CHUNK_SHIFT) > (q_tok >> CHUNK_SHIFT)
        bias[f] = np.where(masked, NEG_BIG, 0.0)
    return bias


def _attn_kernel(sel_ref, ki_ref, bi_ref, qt_ref, k_ref, vt_ref, gate_ref, bias_ref, *rest,
                 n_maps, lambda_init, n_slots, n_qtiles):
    if n_maps == 2:
        lq1_ref, lk1_ref, lq2_ref, lk2_ref, subg_ref, o_ref, s_a, s_b, p_a, p_b, acc_scr = rest
    else:
        o_ref, s_a, s_b, p_a, p_b, acc_scr = rest
    s_buf, p_buf = (s_a, s_b), (p_a, p_b)
    tq_tok = ATTN_COLS // n_maps
    g = pl.program_id(2)
    base = g * n_slots
    tiles = (g, n_qtiles - 1 - g)

    acc_scr[...] = jnp.zeros(acc_scr.shape, F32)
    m = [jnp.full((1, ATTN_COLS), NEG_BIG, F32)] * 2
    l = [jnp.zeros((1, ATTN_COLS), F32)] * 2
    alphas = {}

    def scores(n):
        w, i = sel_ref[base + n], ki_ref[base + n]
        j = jnp.where(w == 0, tiles[0], tiles[1])
        kb = k_ref[0, pl.ds(pl.multiple_of(i * KEY_TILE, KEY_TILE), KEY_TILE), :]
        sv = jnp.dot(kb, qt_ref[0, 0, j], preferred_element_type=F32)
        if n < 2:
            sv = sv + bias_ref[bi_ref[base + n]]
        s_buf[n % 2][...] = sv

    def softmax(n):
        first = sel_ref[base + n] == 0
        sv = s_buf[n % 2][...]
        m_old = jnp.where(first, m[0], m[1])
        m_new = jnp.maximum(m_old, jnp.max(sv, axis=0, keepdims=True))
        alpha = jnp.exp2(m_old - m_new)
        p = jnp.exp2(sv - m_new)
        l_new = alpha * jnp.where(first, l[0], l[1]) + jnp.sum(p, axis=0, keepdims=True)
        p_buf[n % 2][...] = p.astype(BF16)
        alphas[n] = alpha
        m[0], m[1] = jnp.where(first, m_new, m[0]), jnp.where(first, m[1], m_new)
        l[0], l[1] = jnp.where(first, l_new, l[0]), jnp.where(first, l[1], l_new)

    def accumulate(n):
        w, i = sel_ref[base + n], ki_ref[base + n]
        pv = jnp.dot(vt_ref[0, 0, i], p_buf[n % 2][...], preferred_element_type=F32)
        acc_scr[w] = alphas.pop(n) * acc_scr[w] + pv

    for t in range(n_slots + 2):
        if t >= 2:
            accumulate(t - 2)
        if t < n_slots:
            scores(t)
        if 1 <= t <= n_slots:
            softmax(t - 1)

    for w in range(2):
        o_cat = acc_scr[w] / l[w]
        if n_maps == 2:
            f = lambda r: jnp.sum(r, axis=-1, keepdims=True)
            lam = (jnp.exp(f(lq1_ref[...] * lk1_ref[...])) - jnp.exp(f(lq2_ref[...] * lk2_ref[...]))
                   + lambda_init)
            o = (o_cat[:, :tq_tok] - lam * o_cat[:, tq_tok:]).T
            ms = jnp.mean(o * o, axis=-1, keepdims=True)
            o = o * lax.rsqrt(ms + NORM_EPS) * subg_ref[...] * (1.0 - lambda_init)
        else:
            o = o_cat.T
        rows = pl.ds(pl.multiple_of(tiles[w] * tq_tok, tq_tok), tq_tok)
        gt = gate_ref[0, rows, :].astype(F32)
        o_ref[0, rows, :] = (o * (gt * _sigmoid(gt))).astype(o_ref.dtype)


def _attention(qt, k, vt, gate_src, gate_blk0, n_maps, lam_params=(), sub_g=None, lambda_init=0.0):
    bsz, nh, nq, dk, cols = qt.shape
    _, _, nkv, dv, tk = vt.shape
    assert cols == ATTN_COLS and tk == KEY_TILE and nq % 2 == 0
    seq = nkv * tk
    tq_tok = cols // n_maps
    sel, ki, bi, n_slots = _pair_schedule(nq, tq_tok)
    bias = _mask_bias(tq_tok, n_maps)
    in_specs = [
        pl.BlockSpec((1, 1, nq, dk, cols), lambda b, h, g, *_: (b, h, 0, 0, 0)),
        pl.BlockSpec((1, seq, dk), lambda b, h, g, *_: (b, 0, h)),
        pl.BlockSpec((1, 1, nkv, dv, tk), lambda b, h, g, *_: (b, h, 0, 0, 0)),
        pl.BlockSpec((1, seq, dv), lambda b, h, g, *_: (b, 0, gate_blk0 + h)),
        pl.BlockSpec(bias.shape, lambda b, h, g, *_: (0, 0, 0)),
    ]
    args = [qt, k, vt, gate_src, jnp.asarray(bias)]
    if n_maps == 2:
        for p in lam_params:
            in_specs.append(pl.BlockSpec((1, A_QK_DIM), lambda b, h, g, *_: (0, 0)))
            args.append(p.reshape(1, A_QK_DIM))
        in_specs.append(pl.BlockSpec((1, dv), lambda b, h, g, *_: (0, 0)))
        args.append(sub_g.reshape(1, dv))
    return pl.pallas_call(
        functools.partial(_attn_kernel, n_maps=n_maps, lambda_init=lambda_init, n_slots=n_slots, n_qtiles=nq),
        grid_spec=pltpu.PrefetchScalarGridSpec(
            num_scalar_prefetch=3,
            grid=(bsz, nh, nq // 2),
            in_specs=in_specs,
            out_specs=pl.BlockSpec((1, seq, dv), lambda b, h, g, *_: (b, 0, h)),
            scratch_shapes=[
                pltpu.VMEM((tk, cols), F32), pltpu.VMEM((tk, cols), F32),
                pltpu.VMEM((tk, cols), BF16), pltpu.VMEM((tk, cols), BF16),
                pltpu.VMEM((2, dv, cols), F32),
            ],
        ),
        out_shape=jax.ShapeDtypeStruct((bsz, seq, nh * dv), BF16),
        compiler_params=_cparams(("parallel", "parallel", "arbitrary")),
        name="attn%d" % n_maps,
    )(jnp.asarray(sel), jnp.asarray(ki), jnp.asarray(bi), *args)


def _conv_kernel(a_ref, b_ref, gate_ref, w_ref, cb_ref, lng_ref, lnb_ref, o_ref, u_scr, y_scr, sh_scr):
    tc = a_ref.shape[1]
    width = u_scr.shape[1]
    j = pl.program_id(1)

    @pl.when(j == 0)
    def _():
        u_scr[0:CONV_HALO, :] = jnp.zeros((CONV_HALO, width), F32)

    @pl.when(j > 0)
    def _():
        u_scr[0:CONV_HALO, :] = u_scr[tc:tc + CONV_HALO, :]

    u_scr[CONV_HALO:CONV_HALO + tc, :] = a_ref[0].astype(F32) * _sigmoid(b_ref[0].astype(F32))
    first = CONV_HALO - (B_CONV - 1)
    for c0 in range(0, width, LANES):
        cols = slice(c0, c0 + LANES)
        for r0 in range(0, tc, CONV_ROWS):
            acc = jnp.broadcast_to(cb_ref[:, cols], (CONV_ROWS, LANES))
            for res in range(SUBLANES):
                taps = [t for t in range(B_CONV) if (first + t) % SUBLANES == res]
                span = (first + taps[-1]) // SUBLANES * SUBLANES + CONV_ROWS
                sh = sh_scr.at[res % 2]
                sh[0:span, :] = u_scr[r0 + res:r0 + res + span, cols]
                for t in taps:
                    off = (first + t) // SUBLANES * SUBLANES
                    acc = acc + sh[off:off + CONV_ROWS, :] * w_ref[t:t + 1, cols]
            y_scr[r0:r0 + CONV_ROWS, cols] = acc
    acc = y_scr[...]
    mu = jnp.mean(acc, axis=-1, keepdims=True)
    xc = acc - mu
    var = jnp.mean(xc * xc, axis=-1, keepdims=True)
    y = xc * lax.rsqrt(var + NORM_EPS) * lng_ref[...] + lnb_ref[...]
    y = y * _sigmoid(y)
    g = gate_ref[0].astype(F32)
    o_ref[0] = (y * (g * _sigmoid(g))).astype(o_ref.dtype)


def _conv_module(proj, conv_w, conv_b, ln_g, ln_b):
    bsz, seq, w3 = proj.shape
    w = w3 // 3
    tc = min(CONV_TILE, seq)
    blk = lambda c: pl.BlockSpec((1, tc, w), lambda b, j: (b, j, c))
    par = pl.BlockSpec((1, w), lambda b, j: (0, 0))
    return pl.pallas_call(
        _conv_kernel,
        grid=(bsz, seq // tc),
        in_specs=[blk(0), blk(1), blk(2), pl.BlockSpec((B_CONV, w), lambda b, j: (0, 0)), par, par, par],
        out_specs=pl.BlockSpec((1, tc, w), lambda b, j: (b, j, 0)),
        out_shape=jax.ShapeDtypeStruct((bsz, seq, w), BF16),
        scratch_shapes=[pltpu.VMEM((tc + CONV_HALO, w), F32), pltpu.VMEM((tc, w), F32),
                        pltpu.VMEM((2, CONV_ROWS + CONV_HALO, LANES), F32)],
        compiler_params=_cparams(("parallel", "arbitrary")),
        name="conv",
    )(proj, proj, proj, conv_w, conv_b.reshape(1, w), ln_g.reshape(1, w), ln_b.reshape(1, w))


def _c_prep_kernel(p_ref, cos_ref, sin_ref, cost_ref, sint_ref, cqg_ref, ckvg_ref, wuq_ref, wukv_ref,
                   qgn_ref, qgr_ref, kgn_ref, kgr_ref, mean_ref, qt_ref, ko_ref, vt_ref):
    p = p_ref[0].astype(F32)

    def rms(x, g, width):
        ms = jnp.sum(x * x, axis=-1, keepdims=True) * (1.0 / width)
        return x * lax.rsqrt(ms + NORM_EPS) * g

    c_q = rms(p[:, :C_Q_LORA], cqg_ref[...], C_Q_LORA).astype(BF16)
    c_kv = rms(p[:, C_Q_LORA:C_Q_LORA + C_KV_LORA], ckvg_ref[...], C_KV_LORA).astype(BF16)
    q_all = jnp.dot(c_q, wuq_ref[...], preferred_element_type=F32)
    kv_all = jnp.dot(c_kv, wukv_ref[...], preferred_element_type=F32)
    k_rope = _rope64(rms(p[:, C_Q_LORA + C_KV_LORA:], kgr_ref[...], C_ROPE), cos_ref[...], sin_ref[...]).astype(BF16)
    cos_t, sin_t = cost_ref[0:C_ROPE, :], sint_ref[0:C_ROPE, :]
    half = C_ROPE // 2
    zeros_pad = jnp.zeros((C_HEAD_PAD - C_QK, p.shape[0]), BF16)
    for h in range(C_HEADS):
        c0 = h * C_HEAD_PAD
        q_t = q_all[:, c0:c0 + C_HEAD_PAD].T
        qn, qr = q_t[0:C_NOPE], q_t[C_NOPE:C_QK]
        qn = qn * lax.rsqrt(jnp.sum(qn * qn, axis=0, keepdims=True) * (1.0 / C_NOPE) + NORM_EPS) * qgn_ref[...]
        qr = qr * lax.rsqrt(jnp.sum(qr * qr, axis=0, keepdims=True) * (1.0 / C_ROPE) + NORM_EPS) * qgr_ref[...]
        qr = qr * cos_t + jnp.concatenate([qr[half:], qr[:half]], axis=0) * sin_t
        qt_ref[0, h, 0, 0:C_NOPE, :] = qn.astype(BF16)
        qt_ref[0, h, 0, C_NOPE:C_QK, :] = qr.astype(BF16)
        qt_ref[0, h, 0, C_QK:C_HEAD_PAD, :] = zeros_pad
        kn = kv_all[:, c0:c0 + C_NOPE]
        ms = jnp.dot((kn * kn).astype(BF16), mean_ref[...], preferred_element_type=F32)
        ko_ref[0, :, c0:c0 + C_NOPE] = (kn * lax.rsqrt(ms + NORM_EPS) * kgn_ref[...]).astype(BF16)
        ko_ref[0, :, c0 + C_NOPE:c0 + C_HEAD_PAD] = k_rope
        vt_ref[0, h, 0] = kv_all[:, c0 + C_NOPE:c0 + C_HEAD_PAD].T.astype(BF16)


def _c_prep(proj, cos_n, sin_n, cq_g, ckv_g, w_uq, w_ukv, q_g, k_g):
    bsz, seq, _ = proj.shape
    tp = PREP_TILE
    per_q = ATTN_COLS // tp
    kt = KEY_TILE // tp
    nh = C_HEADS
    lead = C_Q_LORA + C_KV_LORA + LANES
    pad_rope = lambda g: jnp.concatenate([g, jnp.zeros((LANES - C_ROPE,), F32)]).reshape(1, LANES)
    scale = C_QK ** -0.5 * LOG2E
    col_gain = lambda g: jnp.broadcast_to((g * scale)[:, None], (g.shape[0], tp))
    wq = w_uq.reshape(C_Q_LORA, nh, C_QK)
    wq = jnp.concatenate([wq, jnp.zeros((C_Q_LORA, nh, C_HEAD_PAD - C_QK), w_uq.dtype)], axis=2)
    wq = wq.reshape(C_Q_LORA, nh * C_HEAD_PAD).astype(BF16)
    full = lambda shape: pl.BlockSpec(shape, lambda b, i: (0,) * len(shape))
    tab = pl.BlockSpec((tp, LANES), lambda b, i: (i, 0))
    tab_t = pl.BlockSpec((LANES, tp), lambda b, i: (0, i))
    return pl.pallas_call(
        _c_prep_kernel,
        grid=(bsz, seq // tp),
        in_specs=[
            pl.BlockSpec((1, tp, lead), lambda b, i: (b, i, 0)),
            tab, tab, tab_t, tab_t,
            full((1, C_Q_LORA)), full((1, C_KV_LORA)),
            full((C_Q_LORA, nh * C_HEAD_PAD)), full((C_KV_LORA, nh * C_HEAD_PAD)),
            full((C_NOPE, tp)), full((C_ROPE, tp)), full((1, C_NOPE)), full((1, LANES)), full((LANES, LANES)),
        ],
        out_specs=[
            pl.BlockSpec((1, nh, 1, C_HEAD_PAD, tp), lambda b, i: (b, 0, i // per_q, 0, i % per_q)),
            pl.BlockSpec((1, tp, nh * C_HEAD_PAD), lambda b, i: (b, i, 0)),
            pl.BlockSpec((1, nh, 1, C_V, tp), lambda b, i: (b, 0, i // kt, 0, i % kt)),
        ],
        out_shape=[
            jax.ShapeDtypeStruct((bsz, nh, seq // ATTN_COLS, C_HEAD_PAD, ATTN_COLS), BF16),
            jax.ShapeDtypeStruct((bsz, seq, nh * C_HEAD_PAD), BF16),
            jax.ShapeDtypeStruct((bsz, nh, seq // KEY_TILE, C_V, KEY_TILE), BF16),
        ],
        compiler_params=_cparams(("parallel", "parallel")),
        name="c_prep",
    )(proj, cos_n, sin_n, cos_n.T, sin_n.T, cq_g.reshape(1, -1), ckv_g.reshape(1, -1), wq, w_ukv.astype(BF16),
      col_gain(q_g[:C_NOPE]), col_gain(q_g[C_NOPE:]), k_g[:C_NOPE].reshape(1, -1), pad_rope(k_g[C_NOPE:]),
      jnp.full((LANES, LANES), 1.0 / C_NOPE, BF16))


def _layer_a(x, tabs, norm_g, w_in, q_g, k_g, lq1, lk1, lq2, lk2, sub_g, w_out, lambda_init):
    bsz, seq, d = x.shape
    x2d = x.reshape(bsz * seq, d)
    n_in = w_in.shape[1]
    proj = _inproj(x2d, norm_g, w_in.astype(BF16), None, 2048).reshape(bsz, seq, n_in)
    qt, k, vt = _a_prep(proj, tabs[0], tabs[1], q_g, k_g)
    gate_blk0 = (n_in - A_HEADS * A_V_DIM) // LANES
    y = _attention(qt, k, vt, proj, gate_blk0, 2, (lq1, lk1, lq2, lk2), sub_g, lambda_init)
    return _outproj(y.reshape(bsz * seq, -1), w_out.astype(BF16), x2d).reshape(bsz, seq, d)


def _layer_b(x, norm_g, w_in, b_in, conv_w, conv_b, ln_g, ln_b, w_out):
    bsz, seq, d = x.shape
    x2d = x.reshape(bsz * seq, d)
    n_in = w_in.shape[1]
    proj = _inproj(x2d, norm_g, w_in.astype(BF16), b_in, 1536).reshape(bsz, seq, n_in)
    y = _conv_module(proj, conv_w, conv_b, ln_g, ln_b)
    return _outproj(y.reshape(bsz * seq, -1), w_out.astype(BF16), x2d).reshape(bsz, seq, d)


def _layer_c(x, tabs, norm_g, w_in, cq_g, w_uq, ckv_g, w_ukv, q_g, k_g, w_out):
    bsz, seq, d = x.shape
    x2d = x.reshape(bsz * seq, d)
    n_lead = C_Q_LORA + C_KV_LORA + C_ROPE
    w = jnp.concatenate([w_in[:, :n_lead], jnp.zeros((d, LANES - C_ROPE), w_in.dtype), w_in[:, n_lead:]], axis=1)
    n_in = w.shape[1]
    proj = _inproj(x2d, norm_g, w.astype(BF16), None, 1536).reshape(bsz, seq, n_in)
    qt, k, vt = _c_prep(proj, tabs[0], tabs[1], cq_g, ckv_g, w_uq, w_ukv, q_g, k_g)
    gate_blk0 = (n_lead + LANES - C_ROPE) // LANES
    y = _attention(qt, k, vt, proj, gate_blk0, 1)
    return _outproj(y.reshape(bsz * seq, -1), w_out.astype(BF16), x2d).reshape(bsz, seq, d)


def kernel(x, a_norm_g, a_w_in, a_q_norm_g, a_k_norm_g, a_lam_q1, a_lam_k1, a_lam_q2, a_lam_k2, a_sub_norm_g, a_w_out, b_norm_g, b_w_in, b_b_in, b_conv_w, b_conv_b, b_ln_g, b_ln_b, b_w_out, c_norm_g, c_w_in, c_cq_norm_g, c_w_uq, c_ckv_norm_g, c_w_ukv, c_q_norm_g, c_k_norm_g, c_w_out):
    depth = a_norm_g.shape[0] + b_norm_g.shape[0] + c_norm_g.shape[0]
    tabs = _rope_tables(x.shape[1])
    for i in range(depth):
        kind, j = i % N_MIXERS, i // N_MIXERS
        if kind == 0:
            lambda_init = 0.8 - 0.6 * math.exp(-0.3 * i)
            x = _layer_a(x, tabs, a_norm_g[j], a_w_in[j], a_q_norm_g[j], a_k_norm_g[j], a_lam_q1[j],
                         a_lam_k1[j], a_lam_q2[j], a_lam_k2[j], a_sub_norm_g[j], a_w_out[j], lambda_init)
        elif kind == 1:
            x = _layer_b(x, b_norm_g[j], b_w_in[j], b_b_in[j], b_conv_w[j], b_conv_b[j], b_ln_g[j],
                         b_ln_b[j], b_w_out[j])
        else:
            x = _layer_c(x, tabs, c_norm_g[j], c_w_in[j], c_cq_norm_g[j], c_w_uq[j], c_ckv_norm_g[j],
                         c_w_ukv[j], c_q_norm_g[j], c_k_norm_g[j], c_w_out[j])
    return x
```
